```python
import jax
import jax.numpy as jnp
from jax import lax
import numpy as np

D_MODEL = 4096
BATCH = 4
SEQ = 2048
DEPTH = 1
DEC_BATCH = 128
DEC_SEQ = 4
PAST_LEN = 2048
PAGE_SIZE = 128

A_HEAD_DIM = 128
A_GROUPS = ((128, 1), (512, 4), (2048, 16))
A_N_GROUPS = 3
A_HEADS_PER_GROUP = 4
A_HEADS = A_N_GROUPS * A_HEADS_PER_GROUP
A_WIDTH = A_HEADS * A_HEAD_DIM
A_OUT_WIDTH = A_HEADS_PER_GROUP * A_HEAD_DIM
ROPE_THETA = 500000.0
ROPE_DIM = A_HEAD_DIM // 4
Q_BLOCK = 128
B_HEAD_DIM = 64
B_WIDTH = D_MODEL // 2
B_HEADS = B_WIDTH // B_HEAD_DIM
DECAY_LORA = 128
AAA_LORA = 128
GATE_LORA = 480
GN_EPS = 64e-5
X_COLS = 3 * A_WIDTH + 2 * D_MODEL
SHIFT_COLS = (B_WIDTH, B_WIDTH, B_WIDTH, DECAY_LORA, AAA_LORA, GATE_LORA)
N_IN_COLS = X_COLS + 3 * B_WIDTH + DECAY_LORA + AAA_LORA + GATE_LORA
N_EXPERT_GROUPS = 8
EXPERTS_PER_GROUP = 8
N_EXPERTS = N_EXPERT_GROUPS * EXPERTS_PER_GROUP
TOP_K = 2
D_EXPERT = 1024
EXPERT_BLOCK = 128
NORM_EPS = 1e-6

kernel_name = 'hybrid_dilated_rwkv7_hmoe_step'


def rmsnorm(x, gain):
    x32 = x.astype(jnp.float32)
    y = x32 * lax.rsqrt(jnp.mean(x32 * x32, axis=-1, keepdims=True) + NORM_EPS)
    return (y * gain.astype(jnp.float32)).astype(x.dtype)


def partial_rotary(x, positions):
    half = ROPE_DIM // 2
    inv_freq = ROPE_THETA ** (-jnp.arange(half, dtype=jnp.float32) / half)
    ang = positions.astype(jnp.float32)[:, None] * inv_freq[None, :]
    cos = jnp.cos(ang)[None, :, None, :]
    sin = jnp.sin(ang)[None, :, None, :]
    xr = x[..., :ROPE_DIM].astype(jnp.float32)
    x1, x2 = xr[..., :half], xr[..., half:]
    rot = jnp.concatenate([x1 * cos - x2 * sin, x2 * cos + x1 * sin], axis=-1).astype(x.dtype)
    return jnp.concatenate([rot, x[..., ROPE_DIM:]], axis=-1)


def dilated_group_attention(q, k_all, v_all, q_idx, window, dilation):
    n_keys = window // dilation + 1
    idx = q_idx[:, None] - dilation * jnp.arange(n_keys, dtype=jnp.int32)[None, :]
    valid = idx >= 0
    idx = jnp.maximum(idx, 0)
    k_sel = k_all[:, idx]
    v_sel = v_all[:, idx]
    s = jnp.einsum('bqhd,bqkhd->bqhk', q, k_sel).astype(jnp.float32) * (A_HEAD_DIM ** -0.5)
    s = jnp.where(valid[None, :, None, :], s, -jnp.inf)
    m = jnp.max(s, axis=-1, keepdims=True)
    e = jnp.exp(s - m)
    denom = jnp.sum(e, axis=-1, keepdims=True)
    o = jnp.einsum('bqhk,bqkhd->bqhd', (e / denom).astype(v_all.dtype), v_sel)
    lse = (m + jnp.log(denom))[..., 0]
    return o, lse


def combine_groups(outs, lses, dtype):
    w = jax.nn.softmax(jnp.stack(lses, axis=0), axis=0)
    o = jnp.einsum('gbqh,gbqhd->bqhd', w, jnp.stack(outs, axis=0).astype(jnp.float32))
    return o.astype(dtype)


def prompt_dilated_attention(q, k, v):
    b, t = q.shape[:2]
    nb = t // Q_BLOCK
    q_blocks = jnp.moveaxis(q.reshape(b, nb, Q_BLOCK, A_N_GROUPS, A_HEADS_PER_GROUP, A_HEAD_DIM), 1, 0)
    starts = jnp.arange(nb, dtype=jnp.int32) * Q_BLOCK

    def block(args):
        q_blk, start = args
        q_idx = start + jnp.arange(Q_BLOCK, dtype=jnp.int32)
        outs, lses = [], []
        for g in range(A_N_GROUPS):
            window, dil = A_GROUPS[g]
            o, l = dilated_group_attention(q_blk[:, :, g], k[:, :, g], v[:, :, g], q_idx, window, dil)
            outs.append(o)
            lses.append(l)
        return combine_groups(outs, lses, q.dtype)

    o = lax.map(block, (q_blocks, starts))
    o = jnp.moveaxis(o, 0, 1).reshape(b, t, A_OUT_WIDTH)
    new_buffers = []
    for g in range(A_N_GROUPS):
        keep = min(A_GROUPS[g][0], t)
        new_buffers.append(jnp.stack([k[:, t - keep:, g], v[:, t - keep:, g]], axis=2))
    return o, new_buffers


def sample_dilated_attention(q, k, v, buffers):
    b, t = q.shape[:2]
    outs, lses, new_buffers = [], [], []
    for g in range(A_N_GROUPS):
        window, dil = A_GROUPS[g]
        buf = buffers[g]
        length = buf.shape[1]
        kv_all = jnp.concatenate([buf.astype(k.dtype), jnp.stack([k[:, :, g], v[:, :, g]], axis=2)], axis=1)
        q_idx = length + jnp.arange(t, dtype=jnp.int32)
        o, l = dilated_group_attention(q[:, :, g], kv_all[:, :, 0], kv_all[:, :, 1], q_idx, window, dil)
        outs.append(o)
        lses.append(l)
        new_buffers.append(kv_all[:, t:])
    o = combine_groups(outs, lses, q.dtype).reshape(b, t, A_OUT_WIDTH)
    return o, new_buffers


def rwkv7_time_mix(h, shift_prev, s0, w_in, mu, w0, w2, a0, a2, g2, k_k, k_a, r_k, ln_w, ln_b):
    b, t, _ = h.shape
    f32 = jnp.float32
    x_prev = jnp.concatenate([shift_prev[:, None].astype(h.dtype), h[:, :-1]], axis=1)
    xx = x_prev - h
    projs = []
    off = X_COLS
    for i in range(len(SHIFT_COLS)):
        width = SHIFT_COLS[i]
        projs.append(jnp.einsum('btd,dc->btc', h + xx * mu[i], w_in[:, off:off + width]))
        off += width
    r, k, v, w_low, a_low, g_low = projs
    w_log = -jax.nn.softplus(-(w0 + jnp.tanh(w_low) @ w2)) - 0.5
    decay = jnp.exp(-jnp.exp(w_log.astype(f32)))
    a = jax.nn.sigmoid(a0 + a_low @ a2)
    g = jax.nn.sigmoid(g_low) @ g2

    def heads(z):
        return z.reshape(b, t, B_HEADS, B_HEAD_DIM)

    kk = heads(k * k_k).astype(f32)
    kk = kk / jnp.maximum(jnp.sqrt(jnp.sum(kk * kk, axis=-1, keepdims=True)), 1e-12)
    k = k * (1 + (a - 1) * k_a)
    r_h, k_h, v_h, a_h = heads(r).astype(f32), heads(k).astype(f32), heads(v).astype(f32), heads(a).astype(f32)
    seq = (r_h, heads(decay), k_h, v_h, -kk, kk * a_h)
    seq = tuple(jnp.moveaxis(z, 1, 0) for z in seq)

    def step(state, inp):
        r_t, w_t, k_t, v_t, a_t, b_t = inp
        sa = jnp.einsum('bhij,bhj->bhi', state, a_t)
        state = state * w_t[:, :, None, :] + sa[..., None] * b_t[:, :, None, :] + v_t[..., None] * k_t[:, :, None, :]
        return state, jnp.einsum('bhij,bhj->bhi', state, r_t)

    s_final, y = lax.scan(step, s0.astype(f32), seq)
    y = jnp.moveaxis(y, 0, 1)
    mean = jnp.mean(y, axis=-1, keepdims=True)
    var = jnp.mean(jnp.square(y - mean), axis=-1, keepdims=True)
    y = ((y - mean) * lax.rsqrt(var + GN_EPS)).reshape(b, t, B_WIDTH) * ln_w.astype(f32) + ln_b.astype(f32)
    bonus = jnp.sum(r_h * k_h * r_k.astype(f32), axis=-1, keepdims=True) * v_h
    out = ((y + bonus.reshape(b, t, B_WIDTH)) * g.astype(f32)).astype(h.dtype)
    return out, h[:, -1], s_final.astype(s0.dtype)


def expert_ffn_grouped(h_flat, expert_id, gate, w_gate, w_up, w_down):
    n_tok, d = h_flat.shape
    n_assign = n_tok * TOP_K
    e_flat = expert_id.reshape(n_assign)
    g_flat = gate.reshape(n_assign)
    tok = jnp.arange(n_assign, dtype=jnp.int32) // TOP_K
    order = jnp.argsort(e_flat)
    e_sorted, tok_sorted, g_sorted = e_flat[order], tok[order], g_flat[order]
    counts = jnp.bincount(e_flat, length=N_EXPERTS)
    padded = (counts + EXPERT_BLOCK - 1) // EXPERT_BLOCK * EXPERT_BLOCK
    start = jnp.cumsum(counts) - counts
    pend = jnp.cumsum(padded)
    pstart = pend - padded
    dest = pstart[e_sorted] + jnp.arange(n_assign, dtype=jnp.int32) - start[e_sorted]
    n_blocks = -(-n_assign // EXPERT_BLOCK) + N_EXPERTS
    x_pad = jnp.zeros((n_blocks * EXPERT_BLOCK, d), h_flat.dtype).at[dest].set(h_flat[tok_sorted])
    block_rows = jnp.arange(n_blocks, dtype=pend.dtype) * EXPERT_BLOCK
    block_expert = jnp.minimum(jnp.searchsorted(pend, block_rows, side='right'), N_EXPERTS - 1)

    def run_block(args):
        xb, e = args
        hid = jax.nn.silu(xb @ w_gate[e]) * (xb @ w_up[e])
        return hid @ w_down[e]

    y_pad = lax.map(run_block, (x_pad.reshape(n_blocks, EXPERT_BLOCK, d), block_expert))
    y_sorted = y_pad.reshape(n_blocks * EXPERT_BLOCK, d)[dest]
    return jax.ops.segment_sum(y_sorted.astype(jnp.float32) * g_sorted[:, None], tok_sorted, num_segments=n_tok)


def hierarchical_moe(h, router_group_w, router_group_b, router_expert_w, router_expert_b, w_gate, w_up, w_down):
    b, t, d = h.shape
    f32 = jnp.float32
    h_flat = h.reshape(b * t, d)
    group_logits = (h_flat @ router_group_w).astype(f32) + router_group_b.astype(f32)
    group = jnp.argmax(group_logits, axis=-1).astype(jnp.int32)
    p_group = jnp.take_along_axis(jax.nn.softmax(group_logits, axis=-1), group[:, None], axis=-1)
    expert_logits = ((h_flat @ router_expert_w).astype(f32) + router_expert_b.astype(f32)).reshape(b * t, N_EXPERT_GROUPS, EXPERTS_PER_GROUP)
    in_group = jnp.take_along_axis(expert_logits, group[:, None, None], axis=1)[:, 0]
    top_val, top_idx = lax.top_k(in_group, TOP_K)
    gate = p_group * jax.nn.softmax(top_val, axis=-1)
    expert_id = group[:, None] * EXPERTS_PER_GROUP + top_idx.astype(jnp.int32)
    out = expert_ffn_grouped(h_flat, expert_id, gate, w_gate, w_up, w_down)
    return out.reshape(b, t, d).astype(h.dtype)


def trunk_layer(x, positions, shift_prev, s0, attention_fn, p):
    b, t, _ = x.shape
    h = rmsnorm(x, p['norm_mix'])
    px = jnp.einsum('btd,dc->btc', h, p['w_in'][:, :X_COLS])
    qkv = px[..., :3 * A_WIDTH].reshape(b, t, 3, A_HEADS, A_HEAD_DIM)
    q = partial_rotary(rmsnorm(qkv[:, :, 0], p['q_norm']), positions)
    k = partial_rotary(rmsnorm(qkv[:, :, 1], p['k_norm']), positions)
    v = qkv[:, :, 2]
    gshape = (b, t, A_N_GROUPS, A_HEADS_PER_GROUP, A_HEAD_DIM)
    a_out, kv_buffers = attention_fn(q.reshape(gshape), k.reshape(gshape), v.reshape(gshape))
    b_out, shift_new, s_new = rwkv7_time_mix(h, shift_prev, s0, p['w_in'], p['mu_shift'], p['rwkv_w0'], p['rwkv_w2'],
                                             p['rwkv_a0'], p['rwkv_a2'], p['rwkv_g2'], p['rwkv_k_k'], p['rwkv_k_a'],
                                             p['rwkv_r_k'], p['rwkv_ln_w'], p['rwkv_ln_b'])
    gate_a = jax.nn.sigmoid(px[..., 3 * A_WIDTH:3 * A_WIDTH + D_MODEL])
    gate_b = jax.nn.sigmoid(px[..., 3 * A_WIDTH + D_MODEL:X_COLS])
    merged = gate_a * (a_out @ p['w_read_a']) + gate_b * (b_out @ p['w_read_b'])
    u = x + merged @ p['w_o']
    y = u + hierarchical_moe(rmsnorm(u, p['norm_ffn']), p['router_group_w'], p['router_group_b'],
                             p['router_expert_w'], p['router_expert_b'], p['expert_w_gate'],
                             p['expert_w_up'], p['expert_w_down'])
    return y, kv_buffers, shift_new, s_new


def setup_inputs(seed: int = 0) -> dict:
    key = jax.random.key(seed)
    ks = iter(jax.random.split(key, 40))
    f32 = jnp.float32

    def nrm(shape, scale):
        return jax.random.normal(next(ks), shape, f32) * scale

    kv_shape = lambda w: (DEC_BATCH, min(w, PAST_LEN), 2, A_HEADS_PER_GROUP, A_HEAD_DIM)
    return {
        'x_prompt': nrm((BATCH, SEQ, D_MODEL), 1.0),
        'x_sample': nrm((DEC_BATCH, DEC_SEQ, D_MODEL), 1.0),
        'cache_kv_w128': nrm(kv_shape(A_GROUPS[0][0]), 1.0),
        'cache_kv_w512': nrm(kv_shape(A_GROUPS[1][0]), 1.0),
        'cache_kv_w2048': nrm(kv_shape(A_GROUPS[2][0]), 1.0),
        'state_shift': nrm((DEC_BATCH, D_MODEL), 1.0),
        'state_wkv': nrm((DEC_BATCH, B_HEADS, B_HEAD_DIM, B_HEAD_DIM), 0.5),
        'norm_mix': 1.0 + nrm((D_MODEL,), 0.02),
        'w_in': nrm((D_MODEL, N_IN_COLS), D_MODEL ** -0.5),
        'q_norm': 1.0 + nrm((A_HEAD_DIM,), 0.02),
        'k_norm': 1.0 + nrm((A_HEAD_DIM,), 0.02),
        'mu_shift': jax.random.uniform(next(ks), (len(SHIFT_COLS), D_MODEL), f32),
        'rwkv_w0': nrm((B_WIDTH,), 0.5),
        'rwkv_w2': nrm((DECAY_LORA, B_WIDTH), 0.1 * DECAY_LORA ** -0.5),
        'rwkv_a0': nrm((B_WIDTH,), 0.1),
        'rwkv_a2': nrm((AAA_LORA, B_WIDTH), 0.1 * AAA_LORA ** -0.5),
        'rwkv_g2': nrm((GATE_LORA, B_WIDTH), GATE_LORA ** -0.5),
        'rwkv_k_k': 0.85 + nrm((B_WIDTH,), 0.05),
        'rwkv_k_a': 1.0 + nrm((B_WIDTH,), 0.05),
        'rwkv_r_k': nrm((B_HEADS, B_HEAD_DIM), 0.1),
        'rwkv_ln_w': 1.0 + nrm((B_WIDTH,), 0.02),
        'rwkv_ln_b': nrm((B_WIDTH,), 0.02),
        'w_read_a': nrm((A_OUT_WIDTH, D_MODEL), A_OUT_WIDTH ** -0.5),
        'w_read_b': nrm((B_WIDTH, D_MODEL), B_WIDTH ** -0.5),
        'w_o': nrm((D_MODEL, D_MODEL), D_MODEL ** -0.5),
        'norm_ffn': 1.0 + nrm((D_MODEL,), 0.02),
        'router_group_w': nrm((D_MODEL, N_EXPERT_GROUPS), D_MODEL ** -0.5),
        'router_group_b': nrm((N_EXPERT_GROUPS,), 0.01),
        'router_expert_w': nrm((D_MODEL, N_EXPERTS), D_MODEL ** -0.5),
        'router_expert_b': nrm((N_EXPERTS,), 0.01),
        'expert_w_gate': nrm((N_EXPERTS, D_MODEL, D_EXPERT), D_MODEL ** -0.5),
        'expert_w_up': nrm((N_EXPERTS, D_MODEL, D_EXPERT), D_MODEL ** -0.5),
        'expert_w_down': nrm((N_EXPERTS, D_EXPERT, D_MODEL), D_EXPERT ** -0.5),
    }


def reference(x_prompt, x_sample, cache_kv_w128, cache_kv_w512, cache_kv_w2048, state_shift, state_wkv,
              norm_mix, w_in, q_norm, k_norm, mu_shift, rwkv_w0, rwkv_w2, rwkv_a0, rwkv_a2, rwkv_g2,
              rwkv_k_k, rwkv_k_a, rwkv_r_k, rwkv_ln_w, rwkv_ln_b, w_read_a, w_read_b, w_o, norm_ffn,
              router_group_w, router_group_b, router_expert_w, router_expert_b,
              expert_w_gate, expert_w_up, expert_w_down):
    p = dict(norm_mix=norm_mix, w_in=w_in, q_norm=q_norm, k_norm=k_norm, mu_shift=mu_shift,
             rwkv_w0=rwkv_w0, rwkv_w2=rwkv_w2, rwkv_a0=rwkv_a0, rwkv_a2=rwkv_a2, rwkv_g2=rwkv_g2,
             rwkv_k_k=rwkv_k_k, rwkv_k_a=rwkv_k_a, rwkv_r_k=rwkv_r_k, rwkv_ln_w=rwkv_ln_w,
             rwkv_ln_b=rwkv_ln_b, w_read_a=w_read_a, w_read_b=w_read_b, w_o=w_o, norm_ffn=norm_ffn,
             router_group_w=router_group_w, router_group_b=router_group_b,
             router_expert_w=router_expert_w, router_expert_b=router_expert_b,
             expert_w_gate=expert_w_gate, expert_w_up=expert_w_up, expert_w_down=expert_w_down)
    bp, tp = x_prompt.shape[:2]
    y_prompt, kv_p, shift_p, wkv_p = trunk_layer(
        x_prompt, jnp.arange(tp, dtype=jnp.int32), jnp.zeros((bp, D_MODEL), x_prompt.dtype),
        jnp.zeros((bp, B_HEADS, B_HEAD_DIM, B_HEAD_DIM), state_wkv.dtype), prompt_dilated_attention, p)
    ts = x_sample.shape[1]
    buffers = (cache_kv_w128, cache_kv_w512, cache_kv_w2048)
    y_sample, kv_s, shift_s, wkv_s = trunk_layer(
        x_sample, PAST_LEN + jnp.arange(ts, dtype=jnp.int32), state_shift, state_wkv,
        lambda q, k, v: sample_dilated_attention(q, k, v, buffers), p)
    return (y_prompt, y_sample, kv_p[0], kv_p[1], kv_p[2], shift_p, wkv_p,
            kv_s[0], kv_s[1], kv_s[2], shift_s, wkv_s)
```

```python
import functools

import jax
import jax.numpy as jnp
from jax import lax
from jax.experimental import pallas as pl
from jax.experimental.pallas import tpu as pltpu

F32 = jnp.float32
BF16 = jnp.bfloat16

LANES = 128
VMEM_LIMIT = 48 * 1024 * 1024

D_MODEL = 4096
HEAD_A = 128
N_GROUPS = 3
HEADS_PER_GROUP = 4
GROUP_W = HEADS_PER_GROUP * HEAD_A
A_WIDTH = N_GROUPS * GROUP_W
DILATIONS = (1, 4, 16)
WINDOW_KEYS = 128
ROPE_THETA = 500000.0
ROPE_DIM = HEAD_A // 4
ROPE_HALF = ROPE_DIM // 2
HEAD_B = 64
B_WIDTH = D_MODEL // 2
DECAY_LORA = 128
AAA_LORA = 128
GATE_LORA = 480
GATE_LORA_PAD = 512
GN_EPS = 64e-5
NORM_EPS = 1e-6
X_COLS = 3 * A_WIDTH + 2 * D_MODEL
N_EXPERT_GROUPS = 8
EXPERTS_PER_GROUP = 8
N_EXPERTS = 64
TOP_K = 2
D_EXPERT = 1024
ROW_BLOCK = 128
ROUTER_PAD = 128
CHUNK = 64
SAMPLE_CHUNK = 8


def _cparams(sem, vmem=VMEM_LIMIT):
    return pltpu.CompilerParams(dimension_semantics=sem, vmem_limit_bytes=vmem)


def _rmsnorm_kernel(x_ref, g_ref, o_ref):
    x = x_ref[...]
    ms = jnp.mean(x * x, axis=-1, keepdims=True)
    o_ref[...] = x * lax.rsqrt(ms + NORM_EPS) * g_ref[...]


def _rmsnorm(x, gain, tm=256):
    n, d = x.shape
    return pl.pallas_call(
        _rmsnorm_kernel,
        out_shape=jax.ShapeDtypeStruct((n, d), F32),
        grid=(n // tm,),
        in_specs=[pl.BlockSpec((tm, d), lambda i: (i, 0)), pl.BlockSpec((1, d), lambda i: (0, 0))],
        out_specs=pl.BlockSpec((tm, d), lambda i: (i, 0)),
        compiler_params=_cparams(("parallel",)),
        name="rmsnorm",
    )(x, gain.reshape(1, d))


def _mix_kernel(h_ref, p_ref, mu_ref, hb_ref, m_ref):
    h = h_ref[...]
    xx = p_ref[...] - h
    hb_ref[...] = h.astype(BF16)
    for i in range(6):
        m_ref[i] = (h + xx * mu_ref[i:i + 1, :]).astype(BF16)


def _token_shift_mix(h, h_prev, mu, tm=128):
    n, d = h.shape
    return pl.pallas_call(
        _mix_kernel,
        out_shape=(jax.ShapeDtypeStruct((n, d), BF16), jax.ShapeDtypeStruct((6, n, d), BF16)),
        grid=(n // tm,),
        in_specs=[pl.BlockSpec((tm, d), lambda i: (i, 0)), pl.BlockSpec((tm, d), lambda i: (i, 0)),
                  pl.BlockSpec((6, d), lambda i: (0, 0))],
        out_specs=(pl.BlockSpec((tm, d), lambda i: (i, 0)), pl.BlockSpec((6, tm, d), lambda i: (0, i, 0))),
        compiler_params=_cparams(("parallel",)),
        name="token_shift_mix",
    )(h, h_prev, mu)


def _mm_kernel(*refs, n_extra, epilogue):
    a_ref, w_ref = refs[0], refs[1]
    extra = refs[2:2 + n_extra]
    o_ref = refs[2 + n_extra]
    acc = jnp.dot(a_ref[...], w_ref[...].astype(BF16), preferred_element_type=F32)
    o_ref[...] = epilogue(acc, *extra).astype(o_ref.dtype)


def _matmul(a, w, *, col_off=0, width=None, tm, tn, epilogue=None, extra=(), extra_specs=(),
            out_dtype=F32, a_sel=None, name="matmul"):
    if a_sel is None:
        n, k = a.shape
        a_spec = pl.BlockSpec((tm, k), lambda j, i: (i, 0))
    else:
        _, n, k = a.shape
        a_spec = pl.BlockSpec((None, tm, k), lambda j, i: (a_sel, i, 0))
    width = w.shape[1] - col_off if width is None else width
    assert col_off % tn == 0 and width % tn == 0 and n % tm == 0
    off = col_off // tn
    if epilogue is None:
        epilogue = lambda acc: acc
    kern = functools.partial(_mm_kernel, n_extra=len(extra), epilogue=epilogue)
    return pl.pallas_call(
        kern,
        out_shape=jax.ShapeDtypeStruct((n, width), out_dtype),
        grid=(width // tn, n // tm),
        in_specs=[a_spec, pl.BlockSpec((k, tn), lambda j, i: (0, j + off))] + list(extra_specs),
        out_specs=pl.BlockSpec((tm, tn), lambda j, i: (i, j)),
        compiler_params=_cparams(("parallel", "parallel")),
        name=name,
    )(a, w, *extra)


def _qk_epilogue(acc, gain_ref, c_ref, a_ref, b_ref):
    c, a, b = c_ref[...], a_ref[...], b_ref[...]
    g = gain_ref[...]
    outs = []
    for hh in range(HEADS_PER_GROUP):
        x = acc[:, hh * HEAD_A:(hh + 1) * HEAD_A]
        ms = jnp.mean(x * x, axis=-1, keepdims=True)
        y = x * lax.rsqrt(ms + NORM_EPS) * g
        y = y * c + pltpu.roll(y, HEAD_A - ROPE_HALF, 1) * a + pltpu.roll(y, ROPE_HALF, 1) * b
        outs.append(y)
    return jnp.concatenate(outs, axis=1)


def _sigmoid_epilogue(acc):
    return jax.nn.sigmoid(acc)


def _rope_tables(positions):
    n = positions.shape[0]
    inv_freq = ROPE_THETA ** (-jnp.arange(ROPE_HALF, dtype=F32) / ROPE_HALF)
    ang = positions.astype(F32)[:, None] * inv_freq[None, :]
    cos, sin = jnp.cos(ang), jnp.sin(ang)
    zeros = lambda w: jnp.zeros((n, w), F32)
    c = jnp.concatenate([cos, cos, jnp.ones((n, HEAD_A - ROPE_DIM), F32)], axis=1)
    a = jnp.concatenate([-sin, zeros(HEAD_A - ROPE_HALF)], axis=1)
    b = jnp.concatenate([zeros(ROPE_HALF), sin, zeros(HEAD_A - ROPE_DIM)], axis=1)
    return c, a, b


def _softmax_parts(s):
    m = jnp.max(s, axis=-1, keepdims=True)
    e = jnp.exp(s - m)
    denom = jnp.sum(e, axis=-1, keepdims=True)
    return e / denom, m + jnp.log(denom)


def _prompt_attn_kernel(q_ref, kp_ref, kc_ref, vp_ref, vc_ref, o_ref, l_ref):
    qb = pl.program_id(2)
    rows = lax.broadcasted_iota(jnp.int32, (ROW_BLOCK, ROW_BLOCK), 0)
    cols = lax.broadcasted_iota(jnp.int32, (ROW_BLOCK, ROW_BLOCK), 1)
    prev_ok = jnp.logical_and(cols >= rows, qb > 0)
    cur_ok = cols <= rows
    mask = jnp.concatenate([prev_ok, cur_ok], axis=1)
    scale = HEAD_A ** -0.5
    for hh in range(HEADS_PER_GROUP):
        sl = slice(hh * HEAD_A, (hh + 1) * HEAD_A)
        q = q_ref[:, sl].astype(BF16)
        k = jnp.concatenate([kp_ref[:, sl], kc_ref[:, sl]], axis=0).astype(BF16)
        v = jnp.concatenate([vp_ref[:, sl], vc_ref[:, sl]], axis=0).astype(BF16)
        s = lax.dot_general(q, k, (((1,), (1,)), ((), ())), preferred_element_type=F32) * scale
        s = jnp.where(mask, s, -jnp.inf)
        p, lse = _softmax_parts(s)
        o_ref[:, sl] = jnp.dot(p.astype(BF16), v, preferred_element_type=F32)
        l_ref[:, sl] = jnp.broadcast_to(lse, (ROW_BLOCK, HEAD_A))


def _prompt_group_attention(qk, v, g, batch, seq):
    d = DILATIONS[g]
    n = qk.shape[0]
    lc = seq // d
    nqb = lc // ROW_BLOCK
    qk_v = qk.reshape(n // d, d * 2 * A_WIDTH)
    v_v = v.reshape(n // d, d * A_WIDTH)
    qcols = 2 * A_WIDTH // GROUP_W
    vcols = A_WIDTH // GROUP_W

    def row(b, qb):
        return b * nqb + qb

    q_spec = pl.BlockSpec((ROW_BLOCK, GROUP_W), lambda b, c, qb: (row(b, qb), c * qcols + g))
    kp_spec = pl.BlockSpec((ROW_BLOCK, GROUP_W), lambda b, c, qb: (row(b, jnp.maximum(qb - 1, 0)), c * qcols + N_GROUPS + g))
    kc_spec = pl.BlockSpec((ROW_BLOCK, GROUP_W), lambda b, c, qb: (row(b, qb), c * qcols + N_GROUPS + g))
    vp_spec = pl.BlockSpec((ROW_BLOCK, GROUP_W), lambda b, c, qb: (row(b, jnp.maximum(qb - 1, 0)), c * vcols + g))
    vc_spec = pl.BlockSpec((ROW_BLOCK, GROUP_W), lambda b, c, qb: (row(b, qb), c * vcols + g))
    o_spec = pl.BlockSpec((ROW_BLOCK, GROUP_W), lambda b, c, qb: (row(b, qb), c))
    out_sds = jax.ShapeDtypeStruct((batch * seq // d, d * GROUP_W), F32)
    o, lse = pl.pallas_call(
        _prompt_attn_kernel,
        out_shape=(out_sds, out_sds),
        grid=(batch, d, nqb),
        in_specs=[q_spec, kp_spec, kc_spec, vp_spec, vc_spec],
        out_specs=(o_spec, o_spec),
        compiler_params=_cparams(("parallel", "parallel", "parallel")),
        name=f"prompt_attn_g{g}",
    )(qk_v, qk_v, qk_v, v_v, v_v)
    return o.reshape(batch * seq, GROUP_W), lse.reshape(batch * seq, GROUP_W)


def _merge_kernel(o0, o1, o2, l0, l1, l2, out_ref):
    ls = [l0[...], l1[...], l2[...]]
    m = jnp.maximum(jnp.maximum(ls[0], ls[1]), ls[2])
    es = [jnp.exp(l - m) for l in ls]
    tot = es[0] + es[1] + es[2]
    acc = (es[0] / tot) * o0[...] + (es[1] / tot) * o1[...] + (es[2] / tot) * o2[...]
    out_ref[...] = acc.astype(out_ref.dtype)


def _merge_groups(outs, lses, tm=512):
    n = outs[0].shape[0]
    spec = pl.BlockSpec((tm, GROUP_W), lambda i: (i, 0))
    return pl.pallas_call(
        _merge_kernel,
        out_shape=jax.ShapeDtypeStruct((n, GROUP_W), BF16),
        grid=(n // tm,),
        in_specs=[spec] * 6,
        out_specs=spec,
        compiler_params=_cparams(("parallel",)),
        name="merge_groups",
    )(*outs, *lses)


SAMPLE_PAIR = 2
NEG_BIG = -1e30


def _sample_attn_kernel(q_ref, k_ref, v_ref, c0_ref, c1_ref, c2_ref, out_ref, *, t_new):
    nrow = SAMPLE_PAIR * t_new
    scale = HEAD_A ** -0.5
    qrow = lax.broadcasted_iota(jnp.int32, (nrow, 1), 0)
    q_elem, q_t = qrow // t_new, qrow % t_new
    res = [None] * HEADS_PER_GROUP
    for hh in range(HEADS_PER_GROUP):
        outs, lses = [], []
        for g in range(N_GROUPS):
            col = (g * HEADS_PER_GROUP + hh) * HEAD_A
            q = q_ref[:, col:col + HEAD_A].astype(BF16)
            k_new = k_ref[:, col:col + HEAD_A]
            v_new = v_ref[:, col:col + HEAD_A]
            o_g, l_g = None, None
            for e in range(SAMPLE_PAIR):
                if g == 0:
                    cache = c0_ref[e]
                    k_c = cache[:, hh * HEAD_A:(hh + 1) * HEAD_A]
                    v_c = cache[:, GROUP_W + hh * HEAD_A:GROUP_W + (hh + 1) * HEAD_A]
                    ncache = WINDOW_KEYS
                else:
                    cache = (c1_ref if g == 1 else c2_ref)[e]
                    kw = 2 * GROUP_W
                    k_c = jnp.concatenate([cache[:, c * kw + hh * HEAD_A:c * kw + (hh + 1) * HEAD_A]
                                           for c in range(t_new)], axis=0)
                    v_c = jnp.concatenate([cache[:, c * kw + GROUP_W + hh * HEAD_A:c * kw + GROUP_W + (hh + 1) * HEAD_A]
                                           for c in range(t_new)], axis=0)
                    ncache = t_new * WINDOW_KEYS
                k_all = jnp.concatenate([k_c, k_new], axis=0).astype(BF16)
                v_all = jnp.concatenate([v_c, v_new], axis=0).astype(BF16)
                s = lax.dot_general(q, k_all, (((1,), (1,)), ((), ())), preferred_element_type=F32) * scale
                kcol = lax.broadcasted_iota(jnp.int32, (1, ncache + nrow), 1)
                is_new = kcol >= ncache
                new_idx = kcol - ncache
                if g == 0:
                    ok_cache = kcol >= q_t
                    ok_new = jnp.logical_and(new_idx // t_new == e, new_idx % t_new <= q_t)
                else:
                    ok_cache = kcol // WINDOW_KEYS == q_t
                    ok_new = jnp.logical_and(new_idx // t_new == e, new_idx % t_new == q_t)
                ok = jnp.logical_or(jnp.logical_and(is_new, ok_new),
                                    jnp.logical_and(jnp.logical_not(is_new), ok_cache))
                ok = jnp.logical_and(ok, q_elem == e)
                s = jnp.where(ok, s, NEG_BIG)
                p, lse = _softmax_parts(s)
                o_e = jnp.dot(p.astype(BF16), v_all, preferred_element_type=F32)
                if e == 0:
                    o_g, l_g = o_e, lse
                else:
                    o_g = jnp.where(q_elem == e, o_e, o_g)
                    l_g = jnp.where(q_elem == e, lse, l_g)
            outs.append(o_g)
            lses.append(l_g)
        m = jnp.maximum(jnp.maximum(lses[0], lses[1]), lses[2])
        es = [jnp.exp(l - m) for l in lses]
        tot = es[0] + es[1] + es[2]
        res[hh] = (es[0] / tot) * outs[0] + (es[1] / tot) * outs[1] + (es[2] / tot) * outs[2]
    out_ref[...] = jnp.concatenate(res, axis=1).astype(out_ref.dtype)


def _sample_attention(qk, v, caches, row0, dec_batch, t_new):
    n = qk.shape[0]
    nrow = SAMPLE_PAIR * t_new
    assert nrow == 8 and row0 % nrow == 0 and dec_batch % SAMPLE_PAIR == 0
    rb0 = row0 // nrow
    kvw = 2 * GROUP_W
    c0 = caches[0].reshape(dec_batch, WINDOW_KEYS, kvw)
    c1 = caches[1].reshape(dec_batch, WINDOW_KEYS, DILATIONS[1] * kvw)
    c2 = caches[2].reshape(dec_batch, WINDOW_KEYS, DILATIONS[2] * kvw)
    assert DILATIONS[1] == t_new
    kern = functools.partial(_sample_attn_kernel, t_new=t_new)
    return pl.pallas_call(
        kern,
        out_shape=jax.ShapeDtypeStruct((dec_batch * t_new, GROUP_W), BF16),
        grid=(dec_batch // SAMPLE_PAIR,),
        in_specs=[pl.BlockSpec((nrow, A_WIDTH), lambda i: (rb0 + i, 0)),
                  pl.BlockSpec((nrow, A_WIDTH), lambda i: (rb0 + i, 1)),
                  pl.BlockSpec((nrow, A_WIDTH), lambda i: (rb0 + i, 0)),
                  pl.BlockSpec((SAMPLE_PAIR, WINDOW_KEYS, kvw), lambda i: (i, 0, 0)),
                  pl.BlockSpec((SAMPLE_PAIR, WINDOW_KEYS, t_new * kvw), lambda i: (i, 0, 0)),
                  pl.BlockSpec((SAMPLE_PAIR, WINDOW_KEYS, t_new * kvw), lambda i: (i, 0, 0))],
        out_specs=pl.BlockSpec((nrow, GROUP_W), lambda i: (i, 0)),
        compiler_params=_cparams(("parallel",)),
        name="sample_attn",
    )(qk, qk, v, c0, c1, c2)


CACHE_SPLIT = 8


def _cache_roll_kernel(*refs):
    caches, news, outs, sem = refs[0:3], refs[3:6], refs[6:9], refs[9]
    copies = []
    for g in range(N_GROUPS):
        length = caches[g].shape[1]
        fresh = news[g].shape[1]
        nb = caches[g].shape[0]
        step = nb // CACHE_SPLIT
        for s in range(CACHE_SPLIT):
            bs = pl.ds(s * step, step)
            copies.append(pltpu.make_async_copy(caches[g].at[bs, pl.ds(fresh, length - fresh)],
                                                outs[g].at[bs, pl.ds(0, length - fresh)],
                                                sem.at[len(copies)]))
        copies.append(pltpu.make_async_copy(news[g], outs[g].at[:, pl.ds(length - fresh, fresh)],
                                            sem.at[len(copies)]))
    for cp in copies:
        cp.start()
    for cp in copies:
        cp.wait()


def _roll_caches(caches, news, t_new):
    rows_per_tok = 2 * GROUP_W // LANES
    c3 = [c.reshape(c.shape[0], c.shape[1] * rows_per_tok, LANES) for c in caches]
    n3 = [x.reshape(x.shape[0], t_new * rows_per_tok, LANES) for x in news]
    any_spec = pl.BlockSpec(memory_space=pl.ANY)
    outs = pl.pallas_call(
        _cache_roll_kernel,
        out_shape=tuple(jax.ShapeDtypeStruct(c.shape, c.dtype) for c in c3),
        in_specs=[any_spec] * 6,
        out_specs=(any_spec,) * 3,
        scratch_shapes=[pltpu.SemaphoreType.DMA((N_GROUPS * (CACHE_SPLIT + 1),))],
        name="cache_roll",
    )(*c3, *n3)
    return [o.reshape(c.shape) for o, c in zip(outs, caches)]


def _split_bf16(x):
    hi = x.astype(BF16)
    lo = (x - hi.astype(F32)).astype(BF16)
    return hi, lo


def _dot3(a, b, dims):
    ah, al = _split_bf16(a)
    bh, bl = _split_bf16(b)
    dn = (dims, ((), ()))
    out = lax.dot_general(ah, bh, dn, preferred_element_type=F32)
    out = out + lax.dot_general(ah, bl, dn, preferred_element_type=F32)
    return out + lax.dot_general(al, bh, dn, preferred_element_type=F32)


def _cumsum_rows(tri, x):
    hi = x.astype(BF16)
    rem = x - hi.astype(F32)
    mid = rem.astype(BF16)
    lo = (rem - mid.astype(F32)).astype(BF16)
    return (jnp.dot(tri, hi, preferred_element_type=F32) + jnp.dot(tri, mid, preferred_element_type=F32)
            + jnp.dot(tri, lo, preferred_element_type=F32))


_NN = ((1,), (0,))
_NT = ((1,), (1,))
_TN = ((0,), (0,))


def _segsum_mat():
    r = lax.broadcasted_iota(jnp.int32, (LANES, LANES), 0) // HEAD_B
    c = lax.broadcasted_iota(jnp.int32, (LANES, LANES), 1) // HEAD_B
    return (r == c).astype(BF16)


def _segsum(x, ones_bd):
    hi, lo = _split_bf16(x)
    return (jnp.dot(hi, ones_bd, preferred_element_type=F32) + jnp.dot(lo, ones_bd, preferred_element_type=F32))


def _rwkv_prep_kernel(k_ref, wl_ref, al_ref, gl_ref, w2_ref, a2_ref, g2_ref, vec_ref,
                      lw_ref, km_ref, na_ref, nb_ref, g_ref):
    w0, a0, k_k, k_a = vec_ref[0:1, :], vec_ref[1:2, :], vec_ref[2:3, :], vec_ref[3:4, :]
    w_lin = w0 + jnp.dot(jnp.tanh(wl_ref[...]).astype(BF16), w2_ref[...].astype(BF16), preferred_element_type=F32)
    w_log = -jax.nn.softplus(-w_lin) - 0.5
    lw_ref[...] = -jnp.exp(w_log)
    a = jax.nn.sigmoid(a0 + jnp.dot(al_ref[...].astype(BF16), a2_ref[...].astype(BF16), preferred_element_type=F32))
    g_ref[...] = jnp.dot(jax.nn.sigmoid(gl_ref[...]).astype(BF16), g2_ref[...].astype(BF16), preferred_element_type=F32)
    k = k_ref[...]
    km_ref[...] = k * (1.0 + (a - 1.0) * k_a)
    kk = k * k_k
    ones_bd = _segsum_mat()
    for s in range(B_WIDTH // LANES):
        sl = slice(s * LANES, (s + 1) * LANES)
        kks = kk[:, sl]
        nrm = jnp.maximum(jnp.sqrt(_segsum(kks * kks, ones_bd)), 1e-12)
        kkn = kks / nrm
        na_ref[:, sl] = -kkn
        nb_ref[:, sl] = kkn * a[:, sl]


def _rwkv_prep(k, w_low, a_low, g_low, w2, a2, g2_pad, vecs, tm=256):
    n = k.shape[0]
    wide = pl.BlockSpec((tm, B_WIDTH), lambda i: (i, 0))
    full = lambda shape: pl.BlockSpec(shape, lambda i: (0, 0))
    sds = jax.ShapeDtypeStruct((n, B_WIDTH), F32)
    return pl.pallas_call(
        _rwkv_prep_kernel,
        out_shape=(sds,) * 5,
        grid=(n // tm,),
        in_specs=[wide, pl.BlockSpec((tm, DECAY_LORA), lambda i: (i, 0)), pl.BlockSpec((tm, AAA_LORA), lambda i: (i, 0)),
                  pl.BlockSpec((tm, GATE_LORA_PAD), lambda i: (i, 0)),
                  full((DECAY_LORA, B_WIDTH)), full((AAA_LORA, B_WIDTH)), full((GATE_LORA_PAD, B_WIDTH)),
                  full((4, B_WIDTH))],
        out_specs=(wide,) * 5,
        compiler_params=_cparams(("parallel",)),
        name="rwkv_prep",
    )(k, w_low, a_low, g_low, w2, a2, g2_pad, vecs)


def _tri_inverse(a_bd, c):
    n = 2 * c
    r = lax.broadcasted_iota(jnp.int32, (n, n), 0)
    q = lax.broadcasted_iota(jnp.int32, (n, n), 1)
    eye = (r == q).astype(F32)
    base = 8
    a0 = jnp.where(r // base == q // base, a_bd, 0.0)
    a2 = _dot3(a0, a0, _NN)
    a4 = _dot3(a2, a2, _NN)
    x = eye + a0
    x = x + _dot3(x, a2, _NN)
    x = x + _dot3(x, a4, _NN)
    size = base
    while size < c:
        off = jnp.where(jnp.logical_and(r // (2 * size) == q // (2 * size), r // size == q // size + 1), a_bd, 0.0)
        x = x + _dot3(_dot3(x, off, _NN), x, _NN)
        size *= 2
    return x


def _rwkv_chunk(r, lw, k, v, a, b, s_bd, c):
    n = 2 * c
    lane = lax.broadcasted_iota(jnp.int32, (c, LANES), 1)
    m0 = (lane < HEAD_B).astype(F32)
    m1 = 1.0 - m0
    stack = lambda x: jnp.concatenate([x * m0, x * m1], axis=0)

    tr = lax.broadcasted_iota(jnp.int32, (c, c), 0)
    tc = lax.broadcasted_iota(jnp.int32, (c, c), 1)
    cl = _cumsum_rows((tr >= tc).astype(BF16), lw)
    cl_end = cl[c - 1:c, :]
    r_t = stack(r * jnp.exp(cl))
    a_t = stack(a * jnp.exp(cl - lw))
    b_t = stack(b * jnp.exp(-cl))
    k_t = stack(k * jnp.exp(-cl))
    b_e = stack(b * jnp.exp(cl_end - cl))
    k_e = stack(k * jnp.exp(cl_end - cl))
    v_s = stack(v)

    rr = lax.broadcasted_iota(jnp.int32, (n, n), 0)
    qq = lax.broadcasted_iota(jnp.int32, (n, n), 1)
    same = rr // c == qq // c
    strict = jnp.logical_and(same, rr > qq)
    incl = jnp.logical_and(same, rr >= qq)
    ar = jnp.concatenate([a_t, r_t], axis=0)
    gb = _dot3(ar, b_t, _NT)
    gk = _dot3(ar, k_t, _NT)
    ab = jnp.where(strict, gb[:n], 0.0)
    rb = jnp.where(incl, gb[n:], 0.0)
    ak = jnp.where(strict, gk[:n], 0.0)
    rk = jnp.where(incl, gk[n:], 0.0)

    t_inv = _tri_inverse(ab, c)
    u0 = _dot3(a_t, s_bd, _NT)
    u = _dot3(t_inv, u0 + _dot3(ak, v_s, _NN), _NN)
    y_s = _dot3(r_t, s_bd, _NT) + _dot3(rb, u, _NN) + _dot3(rk, v_s, _NN)
    y = y_s[:c] + y_s[c:]
    upd = _dot3(jnp.concatenate([u, v_s], axis=0), jnp.concatenate([b_e, k_e], axis=0), _TN)
    s_new = s_bd * jnp.exp(cl_end) + upd
    return y, s_new


def _rwkv_prompt_kernel(r_ref, lw_ref, k_ref, v_ref, a_ref, b_ref, y_ref, s_ref, s_acc):
    ci = pl.program_id(2)

    @pl.when(ci == 0)
    def _():
        s_acc[...] = jnp.zeros_like(s_acc)

    y, s_new = _rwkv_chunk(r_ref[...], lw_ref[...], k_ref[...], v_ref[...], a_ref[...], b_ref[...], s_acc[...], CHUNK)
    y_ref[...] = y
    s_acc[...] = s_new

    @pl.when(ci == pl.num_programs(2) - 1)
    def _():
        s_ref[...] = s_new


def _rwkv_prompt(r, lw, k, v, a, b, batch, seq):
    npair = B_WIDTH // LANES
    nchunk = seq // CHUNK
    spec = pl.BlockSpec((CHUNK, LANES), lambda bi, p, ci: (bi * nchunk + ci, p))
    return pl.pallas_call(
        _rwkv_prompt_kernel,
        out_shape=(jax.ShapeDtypeStruct((batch * seq, B_WIDTH), F32),
                   jax.ShapeDtypeStruct((batch, npair, LANES, LANES), F32)),
        grid=(batch, npair, nchunk),
        in_specs=[spec] * 6,
        out_specs=(spec, pl.BlockSpec((None, None, LANES, LANES), lambda bi, p, ci: (bi, p, 0, 0))),
        scratch_shapes=[pltpu.VMEM((LANES, LANES), F32)],
        compiler_params=_cparams(("parallel", "parallel", "arbitrary")),
        name="rwkv_prompt",
    )(r, lw, k, v, a, b)


def _rwkv_sample_kernel(r_ref, lw_ref, k_ref, v_ref, a_ref, b_ref, s0_ref, y_ref, s_ref, *, t_new):
    c = SAMPLE_CHUNK
    row = lax.broadcasted_iota(jnp.int32, (c, LANES), 0)
    for e in range(SAMPLE_PAIR):
        sel = lambda ref: jnp.where(row < t_new, pltpu.roll(ref[...], (c - e * t_new) % c, 0), 0.0)
        y, s_new = _rwkv_chunk(sel(r_ref), sel(lw_ref), sel(k_ref), sel(v_ref), sel(a_ref), sel(b_ref), s0_ref[e, 0], c)
        s_ref[e, 0] = s_new
        y_back = pltpu.roll(y, e * t_new, 0)
        if e == 0:
            y_all = y_back
        else:
            y_all = jnp.where(row // t_new == e, y_back, y_all)
    y_ref[...] = y_all


def _rwkv_sample(r, lw, k, v, a, b, s0_bd, row0, dec_batch, t_new):
    npair = B_WIDTH // LANES
    nrow = SAMPLE_PAIR * t_new
    rb0 = row0 // nrow
    spec = pl.BlockSpec((nrow, LANES), lambda i, p: (rb0 + i, p))
    s_spec = pl.BlockSpec((SAMPLE_PAIR, 1, LANES, LANES), lambda i, p: (i, p, 0, 0))
    return pl.pallas_call(
        functools.partial(_rwkv_sample_kernel, t_new=t_new),
        out_shape=(jax.ShapeDtypeStruct((dec_batch * t_new, B_WIDTH), F32),
                   jax.ShapeDtypeStruct(s0_bd.shape, F32)),
        grid=(dec_batch // SAMPLE_PAIR, npair),
        in_specs=[spec] * 6 + [s_spec],
        out_specs=(pl.BlockSpec((nrow, LANES), lambda i, p: (i, p)), s_spec),
        compiler_params=_cparams(("parallel", "parallel")),
        name="rwkv_sample",
    )(r, lw, k, v, a, b, s0_bd)


def _rwkv_post_kernel(y_ref, r_ref, k_ref, v_ref, g_ref, vec_ref, o_ref):
    ones_bd = _segsum_mat()
    inv = 1.0 / HEAD_B
    for s in range(B_WIDTH // LANES):
        sl = slice(s * LANES, (s + 1) * LANES)
        y = y_ref[:, sl]
        mean = _segsum(y, ones_bd) * inv
        dlt = y - mean
        var = _segsum(dlt * dlt, ones_bd) * inv
        yn = dlt * lax.rsqrt(var + GN_EPS) * vec_ref[0:1, sl] + vec_ref[1:2, sl]
        bonus = _segsum(r_ref[:, sl] * k_ref[:, sl] * vec_ref[2:3, sl], ones_bd) * v_ref[:, sl]
        o_ref[:, sl] = ((yn + bonus) * g_ref[:, sl]).astype(o_ref.dtype)


def _rwkv_post(y, r, k_mod, v, g, vecs, tm=256):
    n = y.shape[0]
    wide = pl.BlockSpec((tm, B_WIDTH), lambda i: (i, 0))
    return pl.pallas_call(
        _rwkv_post_kernel,
        out_shape=jax.ShapeDtypeStruct((n, B_WIDTH), BF16),
        grid=(n // tm,),
        in_specs=[wide] * 5 + [pl.BlockSpec((3, B_WIDTH), lambda i: (0, 0))],
        out_specs=wide,
        compiler_params=_cparams(("parallel",)),
        name="rwkv_post",
    )(y, r, k_mod, v, g, vecs)


def _read_kernel(ao_ref, bo_ref, wa_ref, wb_ref, ga_ref, gb_ref, o_ref):
    ra = jnp.dot(ao_ref[...], wa_ref[...].astype(BF16), preferred_element_type=F32)
    rb = jnp.dot(bo_ref[...], wb_ref[...].astype(BF16), preferred_element_type=F32)
    o_ref[...] = (ga_ref[...] * ra + gb_ref[...] * rb).astype(o_ref.dtype)


def _gated_read(a_out, b_out, w_read_a, w_read_b, gates, tm, tn=512):
    n = a_out.shape[0]
    ncol = D_MODEL // tn
    return pl.pallas_call(
        _read_kernel,
        out_shape=jax.ShapeDtypeStruct((n, D_MODEL), BF16),
        grid=(ncol, n // tm),
        in_specs=[pl.BlockSpec((tm, GROUP_W), lambda j, i: (i, 0)),
                  pl.BlockSpec((tm, B_WIDTH), lambda j, i: (i, 0)),
                  pl.BlockSpec((GROUP_W, tn), lambda j, i: (0, j)),
                  pl.BlockSpec((B_WIDTH, tn), lambda j, i: (0, j)),
                  pl.BlockSpec((tm, tn), lambda j, i: (i, j)),
                  pl.BlockSpec((tm, tn), lambda j, i: (i, j + ncol))],
        out_specs=pl.BlockSpec((tm, tn), lambda j, i: (i, j)),
        compiler_params=_cparams(("parallel", "parallel")),
        name="gated_read",
    )(a_out, b_out, w_read_a, w_read_b, gates, gates)


def _residual_epilogue(acc, x_ref):
    return x_ref[...] + acc


def _pack_halves(h):
    half = h.shape[1] // 2
    top = pltpu.bitcast(h[:, :half].astype(BF16).astype(F32), jnp.uint32)
    bot = pltpu.bitcast(h[:, half:].astype(BF16).astype(F32), jnp.uint32)
    return top | (bot >> 16)


def _unpack_halves(p):
    top = pltpu.bitcast(p & jnp.uint32(0xFFFF0000), F32).astype(BF16)
    bot = pltpu.bitcast(p << 16, F32).astype(BF16)
    return top, bot


def _ffn_norm_router_kernel(u_ref, g_ref, wr_ref, hb_ref, lg_ref):
    x = u_ref[...]
    ms = jnp.mean(x * x, axis=-1, keepdims=True)
    h = x * lax.rsqrt(ms + NORM_EPS) * g_ref[...]
    hb_ref[...] = _pack_halves(h)
    lg_ref[...] = _dot3(h, wr_ref[...], _NN)


def _ffn_norm_router(u, gain, w_router, tm=256):
    n, d = u.shape
    return pl.pallas_call(
        _ffn_norm_router_kernel,
        out_shape=(jax.ShapeDtypeStruct((n, d // 2), jnp.uint32), jax.ShapeDtypeStruct((n, ROUTER_PAD), F32)),
        grid=(n // tm,),
        in_specs=[pl.BlockSpec((tm, d), lambda i: (i, 0)), pl.BlockSpec((1, d), lambda i: (0, 0)),
                  pl.BlockSpec((d, ROUTER_PAD), lambda i: (0, 0))],
        out_specs=(pl.BlockSpec((tm, d // 2), lambda i: (i, 0)), pl.BlockSpec((tm, ROUTER_PAD), lambda i: (i, 0))),
        compiler_params=_cparams(("parallel",)),
        name="ffn_norm_router",
    )(u, gain.reshape(1, d), w_router)


def _route(logits, group_b, expert_b):
    n = logits.shape[0]
    group_logits = logits[:, :N_EXPERT_GROUPS] + group_b.astype(F32)
    group = jnp.argmax(group_logits, axis=-1).astype(jnp.int32)
    p_group = jnp.take_along_axis(jax.nn.softmax(group_logits, axis=-1), group[:, None], axis=-1)
    expert_logits = (logits[:, N_EXPERT_GROUPS:N_EXPERT_GROUPS + N_EXPERTS] + expert_b.astype(F32)).reshape(
        n, N_EXPERT_GROUPS, EXPERTS_PER_GROUP)
    in_group = jnp.take_along_axis(expert_logits, group[:, None, None], axis=1)[:, 0]
    top_val, top_idx = lax.top_k(in_group, TOP_K)
    gate = p_group * jax.nn.softmax(top_val, axis=-1)
    expert_id = group[:, None] * EXPERTS_PER_GROUP + top_idx.astype(jnp.int32)
    return expert_id, gate


FFN_CHUNKS = 4


def _moe_plan(expert_id):
    n_assign = expert_id.size
    n_blocks = n_assign // ROW_BLOCK + N_EXPERTS
    n_rows = n_blocks * ROW_BLOCK
    max_items = FFN_CHUNKS * n_blocks
    i32 = jnp.int32
    e_flat = expert_id.reshape(n_assign)
    order = jnp.argsort(e_flat).astype(i32)
    e_sorted = e_flat[order]
    counts = jnp.bincount(e_flat, length=N_EXPERTS).astype(i32)
    nblk = (counts + ROW_BLOCK - 1) // ROW_BLOCK
    start = jnp.cumsum(counts) - counts
    blk_start = jnp.cumsum(nblk) - nblk
    dest_sorted = blk_start[e_sorted] * ROW_BLOCK + jnp.arange(n_assign, dtype=i32) - start[e_sorted]
    pos = jnp.zeros((n_assign,), i32).at[order].set(dest_sorted)
    row_src = jnp.zeros((n_rows,), i32).at[dest_sorted].set(order // TOP_K)
    item_cnt = FFN_CHUNKS * nblk
    item_end = jnp.cumsum(item_cnt)
    item_start = item_end - item_cnt
    total = item_end[-1]
    idx = jnp.arange(max_items, dtype=i32)
    ic = jnp.minimum(idx, total - 1)
    it_e = jnp.minimum(jnp.searchsorted(item_end, ic, side='right').astype(i32), N_EXPERTS - 1)
    within = ic - item_start[it_e]
    nb = jnp.maximum(nblk[it_e], 1)
    it_wc = within // nb
    valid = idx < total
    first = jnp.logical_and(valid, within % nb == 0)
    tail = idx - total
    it_blk = jnp.where(valid, blk_start[it_e] + within % nb, total // FFN_CHUNKS + tail // FFN_CHUNKS)
    it_c = jnp.where(valid, it_wc, tail % FFN_CHUNKS)
    return dict(pos=pos, row_src=row_src, it_e=it_e, it_wc=it_wc.astype(i32), it_c=it_c.astype(i32),
                it_blk=it_blk.astype(i32), it_valid=valid.astype(i32), it_first=first.astype(i32),
                n_rows=n_rows, max_items=max_items)


GATHER_ROWS = 128


def _dispatch_kernel(src_ref, x_hbm, o_hbm, sem):
    base = pl.program_id(0) * GATHER_ROWS

    def issue(j, carry):
        pltpu.make_async_copy(x_hbm.at[pl.ds(src_ref[base + j], 1)], o_hbm.at[pl.ds(base + j, 1)], sem).start()
        return carry

    lax.fori_loop(0, GATHER_ROWS, issue, 0)

    def drain(j, carry):
        pltpu.make_async_copy(x_hbm.at[pl.ds(0, 1)], o_hbm.at[pl.ds(base + j, 1)], sem).wait()
        return carry

    lax.fori_loop(0, GATHER_ROWS, drain, 0)


def _dispatch_rows(x, row_src, n_rows):
    d = x.shape[1]
    return pl.pallas_call(
        _dispatch_kernel,
        out_shape=jax.ShapeDtypeStruct((n_rows, d), x.dtype),
        grid_spec=pltpu.PrefetchScalarGridSpec(
            num_scalar_prefetch=1,
            grid=(n_rows // GATHER_ROWS,),
            in_specs=[pl.BlockSpec(memory_space=pl.ANY)],
            out_specs=pl.BlockSpec(memory_space=pl.ANY),
            scratch_shapes=[pltpu.SemaphoreType.DMA(())],
        ),
        compiler_params=_cparams(("arbitrary",)),
        name="moe_dispatch",
    )(row_src, x)


def _ffn_up_kernel(e_ref, wc_ref, c_ref, blk_ref, valid_ref, first_ref, x_ref, wg_ref, wu_ref, h_ref, wg_bf, wu_bf):
    i = pl.program_id(0)

    @pl.when(first_ref[i] == 1)
    def _():
        wg_bf[...] = wg_ref[...].astype(BF16)
        wu_bf[...] = wu_ref[...].astype(BF16)

    @pl.when(valid_ref[i] == 0)
    def _():
        h_ref[...] = jnp.zeros_like(h_ref)

    @pl.when(valid_ref[i] == 1)
    def _():
        top, bot = _unpack_halves(x_ref[...])
        half = top.shape[1]
        gate = (jnp.dot(top, wg_bf[:half, :], preferred_element_type=F32)
                + jnp.dot(bot, wg_bf[half:, :], preferred_element_type=F32))
        up = (jnp.dot(top, wu_bf[:half, :], preferred_element_type=F32)
              + jnp.dot(bot, wu_bf[half:, :], preferred_element_type=F32))
        h_ref[...] = (jax.nn.silu(gate) * up).astype(h_ref.dtype)


def _ffn_down_kernel(e_ref, wc_ref, c_ref, blk_ref, valid_ref, first_ref, h_ref, wd_ref, o_ref, wd_bf):
    i = pl.program_id(0)

    @pl.when(first_ref[i] == 1)
    def _():
        wd_bf[...] = wd_ref[...].astype(BF16)

    @pl.when(valid_ref[i] == 0)
    def _():
        o_ref[...] = jnp.zeros_like(o_ref)

    @pl.when(valid_ref[i] == 1)
    def _():
        o_ref[...] = jnp.dot(h_ref[...], wd_bf[...], preferred_element_type=F32)


def _expert_ffn(x_sorted, plan, w_gate, w_up, w_down):
    n_rows, dh = x_sorted.shape
    d = 2 * dh
    fc = D_EXPERT // FFN_CHUNKS
    dc = d // FFN_CHUNKS
    prefetch = (plan["it_e"], plan["it_wc"], plan["it_c"], plan["it_blk"], plan["it_valid"], plan["it_first"])
    grid = (plan["max_items"],)
    hid = pl.pallas_call(
        _ffn_up_kernel,
        out_shape=jax.ShapeDtypeStruct((n_rows, D_EXPERT), BF16),
        grid_spec=pltpu.PrefetchScalarGridSpec(
            num_scalar_prefetch=len(prefetch),
            grid=grid,
            in_specs=[pl.BlockSpec((ROW_BLOCK, dh), lambda i, e, wc, c, b, v, f: (b[i], 0)),
                      pl.BlockSpec((None, d, fc), lambda i, e, wc, c, b, v, f: (e[i], 0, wc[i])),
                      pl.BlockSpec((None, d, fc), lambda i, e, wc, c, b, v, f: (e[i], 0, wc[i]))],
            out_specs=pl.BlockSpec((ROW_BLOCK, fc), lambda i, e, wc, c, b, v, f: (b[i], c[i])),
            scratch_shapes=[pltpu.VMEM((d, fc), BF16), pltpu.VMEM((d, fc), BF16)],
        ),
        compiler_params=_cparams(("arbitrary",)),
        name="expert_ffn_up",
    )(*prefetch, x_sorted, w_gate, w_up)
    out = pl.pallas_call(
        _ffn_down_kernel,
        out_shape=jax.ShapeDtypeStruct((n_rows, d), F32),
        grid_spec=pltpu.PrefetchScalarGridSpec(
            num_scalar_prefetch=len(prefetch),
            grid=grid,
            in_specs=[pl.BlockSpec((ROW_BLOCK, D_EXPERT), lambda i, e, wc, c, b, v, f: (b[i], 0)),
                      pl.BlockSpec((None, D_EXPERT, dc), lambda i, e, wc, c, b, v, f: (e[i], 0, wc[i]))],
            out_specs=pl.BlockSpec((ROW_BLOCK, dc), lambda i, e, wc, c, b, v, f: (b[i], c[i])),
            scratch_shapes=[pltpu.VMEM((D_EXPERT, dc), BF16)],
        ),
        compiler_params=_cparams(("arbitrary",)),
        name="expert_ffn_down",
    )(*prefetch, hid, w_down)
    return out


COMBINE_ROWS = 64


def _combine_kernel(pos_ref, u_ref, g_ref, y_hbm, o_ref, buf, sem):
    base = pl.program_id(0) * COMBINE_ROWS

    def issue(j, carry):
        for s in range(TOP_K):
            pltpu.make_async_copy(y_hbm.at[pl.ds(pos_ref[(base + j) * TOP_K + s], 1)],
                                  buf.at[s, pl.ds(j, 1)], sem).start()
        return carry

    lax.fori_loop(0, COMBINE_ROWS, issue, 0)

    def drain(j, carry):
        for s in range(TOP_K):
            pltpu.make_async_copy(y_hbm.at[pl.ds(0, 1)], buf.at[s, pl.ds(j, 1)], sem).wait()
        return carry

    lax.fori_loop(0, COMBINE_ROWS, drain, 0)
    g = g_ref[...]
    moe = buf[0] * g[:, 0:1] + buf[1] * g[:, 1:2]
    o_ref[...] = u_ref[...] + moe


def _combine(u, gate, y_sorted, pos):
    n, d = u.shape
    return pl.pallas_call(
        _combine_kernel,
        out_shape=jax.ShapeDtypeStruct((n, d), F32),
        grid_spec=pltpu.PrefetchScalarGridSpec(
            num_scalar_prefetch=1,
            grid=(n // COMBINE_ROWS,),
            in_specs=[pl.BlockSpec((COMBINE_ROWS, d), lambda i, p: (i, 0)),
                      pl.BlockSpec((COMBINE_ROWS, TOP_K), lambda i, p: (i, 0)),
                      pl.BlockSpec(memory_space=pl.ANY)],
            out_specs=pl.BlockSpec((COMBINE_ROWS, d), lambda i, p: (i, 0)),
            scratch_shapes=[pltpu.VMEM((TOP_K, COMBINE_ROWS, d), F32), pltpu.SemaphoreType.DMA(())],
        ),
        compiler_params=_cparams(("arbitrary",)),
        name="moe_combine",
    )(pos, u, gate, y_sorted)


def _pick_tm(n):
    for tm in (1088, 1024, 544, 512, 272, 256, 128):
        if n % tm == 0:
            return tm
    raise ValueError(f"row count {n} is not a multiple of 128")


def kernel(x_prompt, x_sample, cache_kv_w128, cache_kv_w512, cache_kv_w2048, state_shift, state_wkv, norm_mix, w_in, q_norm, k_norm, mu_shift, rwkv_w0, rwkv_w2, rwkv_a0, rwkv_a2, rwkv_g2, rwkv_k_k, rwkv_k_a, rwkv_r_k, rwkv_ln_w, rwkv_ln_b, w_read_a, w_read_b, w_o, norm_ffn, router_group_w, router_group_b, router_expert_w, router_expert_b, expert_w_gate, expert_w_up, expert_w_down):
    batch, seq, d = x_prompt.shape
    dec_batch, t_new, _ = x_sample.shape
    n_p, n_s = batch * seq, dec_batch * t_new
    n = n_p + n_s
    past = cache_kv_w2048.shape[1]
    assert d == D_MODEL and seq % (DILATIONS[2] * ROW_BLOCK) == 0 and seq % CHUNK == 0
    assert cache_kv_w128.shape[1] == 128 and cache_kv_w512.shape[1] == 512 and past == 2048
    tm = _pick_tm(n)

    x = jnp.concatenate([x_prompt.reshape(n_p, d), x_sample.reshape(n_s, d)], axis=0)
    h = _rmsnorm(x, norm_mix)
    h_p = h[:n_p].reshape(batch, seq, d)
    h_s = h[n_p:].reshape(dec_batch, t_new, d)
    prev = jnp.concatenate([
        jnp.concatenate([jnp.zeros((batch, 1, d), F32), h_p[:, :-1]], axis=1).reshape(n_p, d),
        jnp.concatenate([state_shift[:, None].astype(F32), h_s[:, :-1]], axis=1).reshape(n_s, d)], axis=0)
    hb, mixes = _token_shift_mix(h, prev, mu_shift)

    positions = jnp.concatenate([jnp.tile(jnp.arange(seq, dtype=jnp.int32), batch),
                                 jnp.tile(past + jnp.arange(t_new, dtype=jnp.int32), dec_batch)])
    rope_c, rope_a, rope_b = _rope_tables(positions)
    gains = jnp.stack([q_norm, k_norm]).reshape(2, 1, HEAD_A).astype(F32)
    tn = GROUP_W
    row_spec = pl.BlockSpec((tm, HEAD_A), lambda j, i: (i, 0))
    qk = _matmul(hb, w_in, col_off=0, width=2 * A_WIDTH, tm=tm, tn=tn, epilogue=_qk_epilogue,
                 extra=(gains, rope_c, rope_a, rope_b),
                 extra_specs=(pl.BlockSpec((None, 1, HEAD_A), lambda j, i: (j // N_GROUPS, 0, 0)),
                              row_spec, row_spec, row_spec), name="proj_qk")
    v_a = _matmul(hb, w_in, col_off=2 * A_WIDTH, width=A_WIDTH, tm=tm, tn=tn, name="proj_v")
    gates = _matmul(hb, w_in, col_off=3 * A_WIDTH, width=2 * D_MODEL, tm=tm, tn=tn,
                    epilogue=_sigmoid_epilogue, name="proj_gates")
    r_b = _matmul(mixes, w_in, a_sel=0, col_off=X_COLS, width=B_WIDTH, tm=tm, tn=tn, name="proj_r")
    k_b = _matmul(mixes, w_in, a_sel=1, col_off=X_COLS + B_WIDTH, width=B_WIDTH, tm=tm, tn=tn, name="proj_k")
    v_b = _matmul(mixes, w_in, a_sel=2, col_off=X_COLS + 2 * B_WIDTH, width=B_WIDTH, tm=tm, tn=tn, name="proj_vb")
    lora0 = X_COLS + 3 * B_WIDTH
    w_lora = jnp.pad(w_in[:, lora0:], ((0, 0), (0, GATE_LORA_PAD - GATE_LORA)))
    w_low = _matmul(mixes, w_lora, a_sel=3, col_off=0, width=DECAY_LORA, tm=tm, tn=DECAY_LORA, name="proj_wlow")
    a_low = _matmul(mixes, w_lora, a_sel=4, col_off=DECAY_LORA, width=AAA_LORA, tm=tm, tn=AAA_LORA, name="proj_alow")
    g_low = _matmul(mixes, w_lora, a_sel=5, col_off=DECAY_LORA + AAA_LORA, width=GATE_LORA_PAD, tm=tm,
                    tn=DECAY_LORA, name="proj_glow")

    outs, lses = [], []
    for g in range(N_GROUPS):
        o_g, l_g = _prompt_group_attention(qk, v_a, g, batch, seq)
        outs.append(o_g)
        lses.append(l_g)
    a_out_p = _merge_groups(outs, lses)
    caches = (cache_kv_w128, cache_kv_w512, cache_kv_w2048)
    a_out_s = _sample_attention(qk, v_a, caches, n_p, dec_batch, t_new)
    a_out = jnp.concatenate([a_out_p, a_out_s], axis=0)

    def kv_rows(g, rows):
        kk = qk[rows, A_WIDTH + g * GROUP_W:A_WIDTH + (g + 1) * GROUP_W]
        vv = v_a[rows, g * GROUP_W:(g + 1) * GROUP_W]
        return jnp.stack([kk.reshape(-1, HEADS_PER_GROUP, HEAD_A), vv.reshape(-1, HEADS_PER_GROUP, HEAD_A)], axis=1)

    kv_prompt = []
    for g in range(N_GROUPS):
        keep = min(caches[g].shape[1], seq)
        full = kv_rows(g, slice(0, n_p)).reshape(batch, seq, 2, HEADS_PER_GROUP, HEAD_A)
        kv_prompt.append(full[:, seq - keep:])
    news = [kv_rows(g, slice(n_p, n)).reshape(dec_batch, t_new, 2, HEADS_PER_GROUP, HEAD_A) for g in range(N_GROUPS)]
    kv_sample = _roll_caches(caches, news, t_new)

    g2_pad = jnp.pad(rwkv_g2, ((0, GATE_LORA_PAD - GATE_LORA), (0, 0)))
    vec_prep = jnp.stack([rwkv_w0, rwkv_a0, rwkv_k_k, rwkv_k_a]).astype(F32)
    lw, k_mod, neg_kk, kk_a, g_out = _rwkv_prep(k_b, w_low, a_low, g_low, rwkv_w2, rwkv_a2, g2_pad, vec_prep)
    y_p, s_p = _rwkv_prompt(r_b, lw, k_mod, v_b, neg_kk, kk_a, batch, seq)
    npair = B_WIDTH // LANES
    s0 = state_wkv.astype(F32).reshape(dec_batch, npair, 2, HEAD_B, HEAD_B)
    zblk = jnp.zeros_like(s0[:, :, 0])
    s0_bd = jnp.concatenate([jnp.concatenate([s0[:, :, 0], zblk], axis=-1),
                             jnp.concatenate([zblk, s0[:, :, 1]], axis=-1)], axis=-2)
    y_s, s_s = _rwkv_sample(r_b, lw, k_mod, v_b, neg_kk, kk_a, s0_bd, n_p, dec_batch, t_new)

    def unpair(s_bd, nb):
        top = s_bd[:, :, :HEAD_B, :HEAD_B]
        bot = s_bd[:, :, HEAD_B:, HEAD_B:]
        return jnp.stack([top, bot], axis=2).reshape(nb, 2 * npair, HEAD_B, HEAD_B)

    y_b = jnp.concatenate([y_p, y_s], axis=0)
    vec_post = jnp.stack([rwkv_ln_w, rwkv_ln_b, rwkv_r_k.reshape(B_WIDTH)]).astype(F32)
    b_out = _rwkv_post(y_b, r_b, k_mod, v_b, g_out, vec_post)

    merged = _gated_read(a_out, b_out, w_read_a, w_read_b, gates, tm)
    u = _matmul(merged, w_o, tm=tm, tn=tn, epilogue=_residual_epilogue, extra=(x,),
                extra_specs=(pl.BlockSpec((tm, tn), lambda j, i: (i, j)),), name="proj_out")
    w_router = jnp.pad(jnp.concatenate([router_group_w, router_expert_w], axis=1),
                       ((0, 0), (0, ROUTER_PAD - N_EXPERT_GROUPS - N_EXPERTS)))
    hn, logits = _ffn_norm_router(u, norm_ffn, w_router)
    expert_id, gate = _route(logits, router_group_b, router_expert_b)
    plan = _moe_plan(expert_id)
    x_sorted = _dispatch_rows(hn, plan["row_src"], plan["n_rows"])
    y_sorted = _expert_ffn(x_sorted, plan, expert_w_gate, expert_w_up, expert_w_down)
    y = _combine(u, gate, y_sorted, plan["pos"])

    return (y[:n_p].reshape(batch, seq, d), y[n_p:].reshape(dec_batch, t_new, d),
            kv_prompt[0], kv_prompt[1], kv_prompt[2], h_p[:, -1], unpair(s_p, batch).astype(state_wkv.dtype),
            kv_sample[0], kv_sample[1], kv_sample[2], h_s[:, -1], unpair(s_s, dec_batch).astype(state_wkv.dtype))
```

```python
import functools

import jax
import jax.numpy as jnp
from jax import lax
from jax.experimental import pallas as pl
from jax.experimental.pallas import tpu as pltpu

F32 = jnp.float32
BF16 = jnp.bfloat16

LANES = 128
VMEM_LIMIT = 48 * 1024 * 1024

D_MODEL = 4096
HEAD_A = 128
N_GROUPS = 3
HEADS_PER_GROUP = 4
GROUP_W = HEADS_PER_GROUP * HEAD_A
A_WIDTH = N_GROUPS * GROUP_W
DILATIONS = (1, 4, 16)
WINDOW_KEYS = 128
ROPE_THETA = 500000.0
ROPE_DIM = HEAD_A // 4
ROPE_HALF = ROPE_DIM // 2
HEAD_B = 64
B_WIDTH = D_MODEL // 2
DECAY_LORA = 128
AAA_LORA = 128
GATE_LORA = 480
GATE_LORA_PAD = 512
GN_EPS = 64e-5
NORM_EPS = 1e-6
X_COLS = 3 * A_WIDTH + 2 * D_MODEL
N_EXPERT_GROUPS = 8
EXPERTS_PER_GROUP = 8
N_EXPERTS = 64
TOP_K = 2
D_EXPERT = 1024
ROW_BLOCK = 128
ROUTER_PAD = 128
CHUNK = 64
SAMPLE_CHUNK = 8


def _cparams(sem, vmem=VMEM_LIMIT):
    return pltpu.CompilerParams(dimension_semantics=sem, vmem_limit_bytes=vmem)


def _rmsnorm_kernel(x_ref, g_ref, o_ref):
    x = x_ref[...]
    ms = jnp.mean(x * x, axis=-1, keepdims=True)
    o_ref[...] = x * lax.rsqrt(ms + NORM_EPS) * g_ref[...]


def _rmsnorm(x, gain, tm=256):
    n, d = x.shape
    return pl.pallas_call(
        _rmsnorm_kernel,
        out_shape=jax.ShapeDtypeStruct((n, d), F32),
        grid=(n // tm,),
        in_specs=[pl.BlockSpec((tm, d), lambda i: (i, 0)), pl.BlockSpec((1, d), lambda i: (0, 0))],
        out_specs=pl.BlockSpec((tm, d), lambda i: (i, 0)),
        compiler_params=_cparams(("parallel",)),
        name="rmsnorm",
    )(x, gain.reshape(1, d))


def _mix_kernel(h_ref, p_ref, mu_ref, hb_ref, m_ref):
    h = h_ref[...]
    xx = p_ref[...] - h
    hb_ref[...] = h.astype(BF16)
    for i in range(6):
        m_ref[i] = (h + xx * mu_ref[i:i + 1, :]).astype(BF16)


def _token_shift_mix(h, h_prev, mu, tm=128):
    n, d = h.shape
    return pl.pallas_call(
        _mix_kernel,
        out_shape=(jax.ShapeDtypeStruct((n, d), BF16), jax.ShapeDtypeStruct((6, n, d), BF16)),
        grid=(n // tm,),
        in_specs=[pl.BlockSpec((tm, d), lambda i: (i, 0)), pl.BlockSpec((tm, d), lambda i: (i, 0)),
                  pl.BlockSpec((6, d), lambda i: (0, 0))],
        out_specs=(pl.BlockSpec((tm, d), lambda i: (i, 0)), pl.BlockSpec((6, tm, d), lambda i: (0, i, 0))),
        compiler_params=_cparams(("parallel",)),
        name="token_shift_mix",
    )(h, h_prev, mu)


def _mm_kernel(*refs, n_extra, epilogue):
    a_ref, w_ref = refs[0], refs[1]
    extra = refs[2:2 + n_extra]
    o_ref = refs[2 + n_extra]
    acc = jnp.dot(a_ref[...], w_ref[...].astype(BF16), preferred_element_type=F32)
    o_ref[...] = epilogue(acc, *extra).astype(o_ref.dtype)


def _matmul(a, w, *, col_off=0, width=None, tm, tn, epilogue=None, extra=(), extra_specs=(),
            out_dtype=F32, a_sel=None, name="matmul"):
    if a_sel is None:
        n, k = a.shape
        a_spec = pl.BlockSpec((tm, k), lambda j, i: (i, 0))
    else:
        _, n, k = a.shape
        a_spec = pl.BlockSpec((None, tm, k), lambda j, i: (a_sel, i, 0))
    width = w.shape[1] - col_off if width is None else width
    assert col_off % tn == 0 and width % tn == 0 and n % tm == 0
    off = col_off // tn
    if epilogue is None:
        epilogue = lambda acc: acc
    kern = functools.partial(_mm_kernel, n_extra=len(extra), epilogue=epilogue)
    return pl.pallas_call(
        kern,
        out_shape=jax.ShapeDtypeStruct((n, width), out_dtype),
        grid=(width // tn, n // tm),
        in_specs=[a_spec, pl.BlockSpec((k, tn), lambda j, i: (0, j + off))] + list(extra_specs),
        out_specs=pl.BlockSpec((tm, tn), lambda j, i: (i, j)),
        compiler_params=_cparams(("parallel", "parallel")),
        name=name,
    )(a, w, *extra)


def _qk_epilogue(acc, gain_ref, c_ref, a_ref, b_ref):
    c, a, b = c_ref[...], a_ref[...], b_ref[...]
    g = gain_ref[...]
    outs = []
    for hh in range(HEADS_PER_GROUP):
        x = acc[:, hh * HEAD_A:(hh + 1) * HEAD_A]
        ms = jnp.mean(x * x, axis=-1, keepdims=True)
        y = x * lax.rsqrt(ms + NORM_EPS) * g
        y = y * c + pltpu.roll(y, HEAD_A - ROPE_HALF, 1) * a + pltpu.roll(y, ROPE_HALF, 1) * b
        outs.append(y)
    return jnp.concatenate(outs, axis=1)


def _sigmoid_epilogue(acc):
    return jax.nn.sigmoid(acc)


def _rope_tables(positions):
    n = positions.shape[0]
    inv_freq = ROPE_THETA ** (-jnp.arange(ROPE_HALF, dtype=F32) / ROPE_HALF)
    ang = positions.astype(F32)[:, None] * inv_freq[None, :]
    cos, sin = jnp.cos(ang), jnp.sin(ang)
    zeros = lambda w: jnp.zeros((n, w), F32)
    c = jnp.concatenate([cos, cos, jnp.ones((n, HEAD_A - ROPE_DIM), F32)], axis=1)
    a = jnp.concatenate([-sin, zeros(HEAD_A - ROPE_HALF)], axis=1)
    b = jnp.concatenate([zeros(ROPE_HALF), sin, zeros(HEAD_A - ROPE_DIM)], axis=1)
    return c, a, b


def _softmax_parts(s):
    m = jnp.max(s, axis=-1, keepdims=True)
    e = jnp.exp(s - m)
    denom = jnp.sum(e, axis=-1, keepdims=True)
    return e / denom, m + jnp.log(denom)


def _prompt_attn_kernel(q_ref, kp_ref, kc_ref, vp_ref, vc_ref, o_ref, l_ref):
    qb = pl.program_id(2)
    rows = lax.broadcasted_iota(jnp.int32, (ROW_BLOCK, ROW_BLOCK), 0)
    cols = lax.broadcasted_iota(jnp.int32, (ROW_BLOCK, ROW_BLOCK), 1)
    prev_ok = jnp.logical_and(cols >= rows, qb > 0)
    cur_ok = cols <= rows
    mask = jnp.concatenate([prev_ok, cur_ok], axis=1)
    scale = HEAD_A ** -0.5
    for hh in range(HEADS_PER_GROUP):
        sl = slice(hh * HEAD_A, (hh + 1) * HEAD_A)
        q = q_ref[:, sl].astype(BF16)
        k = jnp.concatenate([kp_ref[:, sl], kc_ref[:, sl]], axis=0).astype(BF16)
        v = jnp.concatenate([vp_ref[:, sl], vc_ref[:, sl]], axis=0).astype(BF16)
        s = lax.dot_general(q, k, (((1,), (1,)), ((), ())), preferred_element_type=F32) * scale
        s = jnp.where(mask, s, -jnp.inf)
        p, lse = _softmax_parts(s)
        o_ref[:, sl] = jnp.dot(p.astype(BF16), v, preferred_element_type=F32)
        l_ref[:, sl] = jnp.broadcast_to(lse, (ROW_BLOCK, HEAD_A))


def _prompt_group_attention(qk, v, g, batch, seq):
    d = DILATIONS[g]
    n = qk.shape[0]
    lc = seq // d
    nqb = lc // ROW_BLOCK
    qk_v = qk.reshape(n // d, d * 2 * A_WIDTH)
    v_v = v.reshape(n // d, d * A_WIDTH)
    qcols = 2 * A_WIDTH // GROUP_W
    vcols = A_WIDTH // GROUP_W

    def row(b, qb):
        return b * nqb + qb

    q_spec = pl.BlockSpec((ROW_BLOCK, GROUP_W), lambda b, c, qb: (row(b, qb), c * qcols + g))
    kp_spec = pl.BlockSpec((ROW_BLOCK, GROUP_W), lambda b, c, qb: (row(b, jnp.maximum(qb - 1, 0)), c * qcols + N_GROUPS + g))
    kc_spec = pl.BlockSpec((ROW_BLOCK, GROUP_W), lambda b, c, qb: (row(b, qb), c * qcols + N_GROUPS + g))
    vp_spec = pl.BlockSpec((ROW_BLOCK, GROUP_W), lambda b, c, qb: (row(b, jnp.maximum(qb - 1, 0)), c * vcols + g))
    vc_spec = pl.BlockSpec((ROW_BLOCK, GROUP_W), lambda b, c, qb: (row(b, qb), c * vcols + g))
    o_spec = pl.BlockSpec((ROW_BLOCK, GROUP_W), lambda b, c, qb: (row(b, qb), c))
    out_sds = jax.ShapeDtypeStruct((batch * seq // d, d * GROUP_W), F32)
    o, lse = pl.pallas_call(
        _prompt_attn_kernel,
        out_shape=(out_sds, out_sds),
        grid=(batch, d, nqb),
        in_specs=[q_spec, kp_spec, kc_spec, vp_spec, vc_spec],
        out_specs=(o_spec, o_spec),
        compiler_params=_cparams(("parallel", "parallel", "parallel")),
        name=f"prompt_attn_g{g}",
    )(qk_v, qk_v, qk_v, v_v, v_v)
    return o.reshape(batch * seq, GROUP_W), lse.reshape(batch * seq, GROUP_W)


def _merge_kernel(o0, o1, o2, l0, l1, l2, out_ref):
    ls = [l0[...], l1[...], l2[...]]
    m = jnp.maximum(jnp.maximum(ls[0], ls[1]), ls[2])
    es = [jnp.exp(l - m) for l in ls]
    tot = es[0] + es[1] + es[2]
    acc = (es[0] / tot) * o0[...] + (es[1] / tot) * o1[...] + (es[2] / tot) * o2[...]
    out_ref[...] = acc.astype(out_ref.dtype)


def _merge_groups(outs, lses, tm=512):
    n = outs[0].shape[0]
    spec = pl.BlockSpec((tm, GROUP_W), lambda i: (i, 0))
    return pl.pallas_call(
        _merge_kernel,
        out_shape=jax.ShapeDtypeStruct((n, GROUP_W), BF16),
        grid=(n // tm,),
        in_specs=[spec] * 6,
        out_specs=spec,
        compiler_params=_cparams(("parallel",)),
        name="merge_groups",
    )(*outs, *lses)


SAMPLE_PAIR = 2
NEG_BIG = -1e30
SAMPLE_ATTN_VMEM = 56 * 1024 * 1024


def _sample_attn_kernel(q_ref, k_ref, v_ref, c0_ref, c1_ref, c2_ref, out_ref, *, t_new):
    nrow = SAMPLE_PAIR * t_new
    scale = HEAD_A ** -0.5
    qrow = lax.broadcasted_iota(jnp.int32, (nrow, 1), 0)
    q_elem, q_t = qrow // t_new, qrow % t_new
    res = [None] * HEADS_PER_GROUP
    for hh in range(HEADS_PER_GROUP):
        outs, lses = [], []
        for g in range(N_GROUPS):
            col = (g * HEADS_PER_GROUP + hh) * HEAD_A
            q = q_ref[:, col:col + HEAD_A].astype(BF16)
            k_new = k_ref[:, col:col + HEAD_A]
            v_new = v_ref[:, col:col + HEAD_A]
            o_g, l_g = None, None
            for e in range(SAMPLE_PAIR):
                if g == 0:
                    k_c = c0_ref[e, :, 0, hh, :]
                    v_c = c0_ref[e, :, 1, hh, :]
                    ncache = WINDOW_KEYS
                else:
                    cref = c1_ref if g == 1 else c2_ref
                    k_c = jnp.concatenate([cref[e, :, c, 0, hh, :] for c in range(t_new)], axis=0)
                    v_c = jnp.concatenate([cref[e, :, c, 1, hh, :] for c in range(t_new)], axis=0)
                    ncache = t_new * WINDOW_KEYS
                k_all = jnp.concatenate([k_c, k_new], axis=0).astype(BF16)
                v_all = jnp.concatenate([v_c, v_new], axis=0).astype(BF16)
                s = lax.dot_general(q, k_all, (((1,), (1,)), ((), ())), preferred_element_type=F32) * scale
                kcol = lax.broadcasted_iota(jnp.int32, (1, ncache + nrow), 1)
                is_new = kcol >= ncache
                new_idx = kcol - ncache
                if g == 0:
                    ok_cache = kcol >= q_t
                    ok_new = jnp.logical_and(new_idx // t_new == e, new_idx % t_new <= q_t)
                else:
                    ok_cache = kcol // WINDOW_KEYS == q_t
                    ok_new = jnp.logical_and(new_idx // t_new == e, new_idx % t_new == q_t)
                ok = jnp.logical_or(jnp.logical_and(is_new, ok_new),
                                    jnp.logical_and(jnp.logical_not(is_new), ok_cache))
                ok = jnp.logical_and(ok, q_elem == e)
                s = jnp.where(ok, s, NEG_BIG)
                p, lse = _softmax_parts(s)
                o_e = jnp.dot(p.astype(BF16), v_all, preferred_element_type=F32)
                if e == 0:
                    o_g, l_g = o_e, lse
                else:
                    o_g = jnp.where(q_elem == e, o_e, o_g)
                    l_g = jnp.where(q_elem == e, lse, l_g)
            outs.append(o_g)
            lses.append(l_g)
        m = jnp.maximum(jnp.maximum(lses[0], lses[1]), lses[2])
        es = [jnp.exp(l - m) for l in lses]
        tot = es[0] + es[1] + es[2]
        res[hh] = (es[0] / tot) * outs[0] + (es[1] / tot) * outs[1] + (es[2] / tot) * outs[2]
    out_ref[...] = jnp.concatenate(res, axis=1).astype(out_ref.dtype)


def _sample_attention(qk, v, caches, row0, dec_batch, t_new):
    n = qk.shape[0]
    nrow = SAMPLE_PAIR * t_new
    assert nrow == 8 and row0 % nrow == 0 and dec_batch % SAMPLE_PAIR == 0
    rb0 = row0 // nrow
    tail = caches[0].shape[2:]
    c1 = caches[1].reshape((dec_batch, WINDOW_KEYS, DILATIONS[1]) + tail)
    c2 = caches[2].reshape((dec_batch, WINDOW_KEYS, DILATIONS[2]) + tail)
    assert DILATIONS[1] == t_new
    kern = functools.partial(_sample_attn_kernel, t_new=t_new)
    class_blk = (SAMPLE_PAIR, WINDOW_KEYS, t_new) + tail
    return pl.pallas_call(
        kern,
        out_shape=jax.ShapeDtypeStruct((dec_batch * t_new, GROUP_W), BF16),
        grid=(dec_batch // SAMPLE_PAIR,),
        in_specs=[pl.BlockSpec((nrow, A_WIDTH), lambda i: (rb0 + i, 0)),
                  pl.BlockSpec((nrow, A_WIDTH), lambda i: (rb0 + i, 1)),
                  pl.BlockSpec((nrow, A_WIDTH), lambda i: (rb0 + i, 0)),
                  pl.BlockSpec((SAMPLE_PAIR, WINDOW_KEYS) + tail, lambda i: (i, 0, 0, 0, 0)),
                  pl.BlockSpec(class_blk, lambda i: (i, 0, 0, 0, 0, 0)),
                  pl.BlockSpec(class_blk, lambda i: (i, 0, 0, 0, 0, 0))],
        out_specs=pl.BlockSpec((nrow, GROUP_W), lambda i: (i, 0)),
        compiler_params=_cparams(("parallel",), vmem=SAMPLE_ATTN_VMEM),
        name="sample_attn",
    )(qk, qk, v, caches[0], c1, c2)


def _cache_roll_kernel(c_ref, n_ref, o_ref):
    last = pl.program_id(0) == pl.num_programs(0) - 1
    o_ref[...] = jnp.where(last, n_ref[...], c_ref[...])


def _roll_cache(cache, new, t_new):
    nb, length = cache.shape[0], cache.shape[1]
    assert length % t_new == 0
    nblk = length // t_new
    blk = (nb, t_new) + cache.shape[2:]
    return pl.pallas_call(
        _cache_roll_kernel,
        out_shape=jax.ShapeDtypeStruct(cache.shape, cache.dtype),
        grid=(nblk,),
        in_specs=[pl.BlockSpec(blk, lambda j: (0, jnp.minimum(j + 1, nblk - 1), 0, 0, 0)),
                  pl.BlockSpec(blk, lambda j: (0, 0, 0, 0, 0))],
        out_specs=pl.BlockSpec(blk, lambda j: (0, j, 0, 0, 0)),
        compiler_params=_cparams(("parallel",)),
        name=f"cache_roll_w{length}",
    )(cache, new)


def _roll_caches(caches, news, t_new):
    return [_roll_cache(c, x.astype(c.dtype), t_new) for c, x in zip(caches, news)]


def _split_bf16(x):
    hi = x.astype(BF16)
    lo = (x - hi.astype(F32)).astype(BF16)
    return hi, lo


def _dot3(a, b, dims):
    ah, al = _split_bf16(a)
    bh, bl = _split_bf16(b)
    dn = (dims, ((), ()))
    out = lax.dot_general(ah, bh, dn, preferred_element_type=F32)
    out = out + lax.dot_general(ah, bl, dn, preferred_element_type=F32)
    return out + lax.dot_general(al, bh, dn, preferred_element_type=F32)


def _cumsum_rows(tri, x):
    hi = x.astype(BF16)
    rem = x - hi.astype(F32)
    mid = rem.astype(BF16)
    lo = (rem - mid.astype(F32)).astype(BF16)
    return (jnp.dot(tri, hi, preferred_element_type=F32) + jnp.dot(tri, mid, preferred_element_type=F32)
            + jnp.dot(tri, lo, preferred_element_type=F32))


_NN = ((1,), (0,))
_NT = ((1,), (1,))
_TN = ((0,), (0,))


def _segsum_mat():
    r = lax.broadcasted_iota(jnp.int32, (LANES, LANES), 0) // HEAD_B
    c = lax.broadcasted_iota(jnp.int32, (LANES, LANES), 1) // HEAD_B
    return (r == c).astype(BF16)


def _segsum(x, ones_bd):
    hi, lo = _split_bf16(x)
    return (jnp.dot(hi, ones_bd, preferred_element_type=F32) + jnp.dot(lo, ones_bd, preferred_element_type=F32))


def _rwkv_prep_kernel(k_ref, wl_ref, al_ref, gl_ref, w2_ref, a2_ref, g2_ref, vec_ref,
                      lw_ref, km_ref, na_ref, nb_ref, g_ref):
    w0, a0, k_k, k_a = vec_ref[0:1, :], vec_ref[1:2, :], vec_ref[2:3, :], vec_ref[3:4, :]
    w_lin = w0 + jnp.dot(jnp.tanh(wl_ref[...]).astype(BF16), w2_ref[...].astype(BF16), preferred_element_type=F32)
    w_log = -jax.nn.softplus(-w_lin) - 0.5
    lw_ref[...] = -jnp.exp(w_log)
    a = jax.nn.sigmoid(a0 + jnp.dot(al_ref[...].astype(BF16), a2_ref[...].astype(BF16), preferred_element_type=F32))
    g_ref[...] = jnp.dot(jax.nn.sigmoid(gl_ref[...]).astype(BF16), g2_ref[...].astype(BF16), preferred_element_type=F32)
    k = k_ref[...]
    km_ref[...] = k * (1.0 + (a - 1.0) * k_a)
    kk = k * k_k
    ones_bd = _segsum_mat()
    for s in range(B_WIDTH // LANES):
        sl = slice(s * LANES, (s + 1) * LANES)
        kks = kk[:, sl]
        nrm = jnp.maximum(jnp.sqrt(_segsum(kks * kks, ones_bd)), 1e-12)
        kkn = kks / nrm
        na_ref[:, sl] = -kkn
        nb_ref[:, sl] = kkn * a[:, sl]


def _rwkv_prep(k, w_low, a_low, g_low, w2, a2, g2_pad, vecs, tm=256):
    n = k.shape[0]
    wide = pl.BlockSpec((tm, B_WIDTH), lambda i: (i, 0))
    full = lambda shape: pl.BlockSpec(shape, lambda i: (0, 0))
    sds = jax.ShapeDtypeStruct((n, B_WIDTH), F32)
    return pl.pallas_call(
        _rwkv_prep_kernel,
        out_shape=(sds,) * 5,
        grid=(n // tm,),
        in_specs=[wide, pl.BlockSpec((tm, DECAY_LORA), lambda i: (i, 0)), pl.BlockSpec((tm, AAA_LORA), lambda i: (i, 0)),
                  pl.BlockSpec((tm, GATE_LORA_PAD), lambda i: (i, 0)),
                  full((DECAY_LORA, B_WIDTH)), full((AAA_LORA, B_WIDTH)), full((GATE_LORA_PAD, B_WIDTH)),
                  full((4, B_WIDTH))],
        out_specs=(wide,) * 5,
        compiler_params=_cparams(("parallel",)),
        name="rwkv_prep",
    )(k, w_low, a_low, g_low, w2, a2, g2_pad, vecs)


def _each(fn, *lists):
    return [fn(*args) for args in zip(*lists)]


def _dot1(a, b, dims):
    return lax.dot_general(a.astype(BF16), b.astype(BF16), (dims, ((), ())), preferred_element_type=F32)


def _mm_each(xs, ys, dims):
    return _each(lambda x, y: _dot1(x, y, dims), xs, ys)


def _tri_inverse(a_bds, c):
    n = 2 * c
    r = lax.broadcasted_iota(jnp.int32, (n, n), 0)
    q = lax.broadcasted_iota(jnp.int32, (n, n), 1)
    eye = (r == q).astype(F32)
    base = 8
    in_base = r // base == q // base
    a0 = [jnp.where(in_base, a, 0.0) for a in a_bds]
    a2 = _mm_each(a0, a0, _NN)
    a4 = _mm_each(a2, a2, _NN)
    x = [eye + a for a in a0]
    x = _each(jnp.add, x, _mm_each(x, a2, _NN))
    x = _each(jnp.add, x, _mm_each(x, a4, _NN))
    size = base
    while size < c:
        below = jnp.logical_and(r // (2 * size) == q // (2 * size), r // size == q // size + 1)
        off = [jnp.where(below, a, 0.0) for a in a_bds]
        x = _each(jnp.add, x, _mm_each(_mm_each(x, off, _NN), x, _NN))
        size *= 2
    return x


def _rwkv_chunk(chains, c):
    n = 2 * c
    lane = lax.broadcasted_iota(jnp.int32, (c, LANES), 1)
    m0 = (lane < HEAD_B).astype(F32)
    m1 = 1.0 - m0
    stack = lambda x: jnp.concatenate([x * m0, x * m1], axis=0)
    r, lw, k, v, a, b, s_bd = (list(col) for col in zip(*chains))

    tr = lax.broadcasted_iota(jnp.int32, (c, c), 0)
    tc = lax.broadcasted_iota(jnp.int32, (c, c), 1)
    tri = (tr >= tc).astype(BF16)
    cl = [_cumsum_rows(tri, x) for x in lw]
    cl_end = [x[c - 1:c, :] for x in cl]
    r_t = _each(lambda x, d: stack(x * jnp.exp(d)), r, cl)
    a_t = _each(lambda x, d, w: stack(x * jnp.exp(d - w)), a, cl, lw)
    b_t = _each(lambda x, d: stack(x * jnp.exp(-d)), b, cl)
    k_t = _each(lambda x, d: stack(x * jnp.exp(-d)), k, cl)
    bk_e = _each(lambda x, y, d, e: jnp.concatenate([stack(x * jnp.exp(e - d)), stack(y * jnp.exp(e - d))], axis=0),
                 b, k, cl, cl_end)
    v_s = [stack(x) for x in v]

    rr = lax.broadcasted_iota(jnp.int32, (n, n), 0)
    qq = lax.broadcasted_iota(jnp.int32, (n, n), 1)
    same = rr // c == qq // c
    strict = jnp.logical_and(same, rr > qq)
    incl = jnp.logical_and(same, rr >= qq)
    ar = _each(lambda x, y: jnp.concatenate([x, y], axis=0), a_t, r_t)
    gb = _mm_each(ar, b_t, _NT)
    gk = _mm_each(ar, k_t, _NT)
    ab = [jnp.where(strict, g[:n], 0.0) for g in gb]
    rb = [jnp.where(incl, g[n:], 0.0) for g in gb]
    ak = [jnp.where(strict, g[:n], 0.0) for g in gk]
    rk = [jnp.where(incl, g[n:], 0.0) for g in gk]

    us = _mm_each(ar, s_bd, _NT)
    akv = _mm_each(ak, v_s, _NN)
    rkv = _mm_each(rk, v_s, _NN)
    t_inv = _tri_inverse(ab, c)
    u = _mm_each(t_inv, _each(lambda x, y: x[:n] + y, us, akv), _NN)
    rbu = _mm_each(rb, u, _NN)
    y_s = _each(lambda x, y, z: x[n:] + y + z, us, rbu, rkv)
    upd = _mm_each(_each(lambda x, y: jnp.concatenate([x, y], axis=0), u, v_s), bk_e, _TN)
    s_new = _each(lambda s, e, d: s * jnp.exp(e) + d, s_bd, cl_end, upd)
    return [(y[:c] + y[c:], s) for y, s in zip(y_s, s_new)]


RWKV_PAIRS = 16


def _rwkv_prompt_kernel(r_ref, lw_ref, k_ref, v_ref, a_ref, b_ref, y_ref, s_ref, s_acc):
    ci = pl.program_id(2)

    @pl.when(ci == 0)
    def _():
        s_acc[...] = jnp.zeros_like(s_acc)

    lanes = [slice(p * LANES, (p + 1) * LANES) for p in range(RWKV_PAIRS)]
    chains = [(r_ref[:, sl], lw_ref[:, sl], k_ref[:, sl], v_ref[:, sl], a_ref[:, sl], b_ref[:, sl], s_acc[p])
              for p, sl in enumerate(lanes)]
    res = _rwkv_chunk(chains, CHUNK)
    for p, sl in enumerate(lanes):
        y_ref[:, sl] = res[p][0]
        s_acc[p] = res[p][1]

    @pl.when(ci == pl.num_programs(2) - 1)
    def _():
        s_ref[...] = s_acc[...]


def _rwkv_prompt(r, lw, k, v, a, b, batch, seq):
    npair = B_WIDTH // LANES
    nchunk = seq // CHUNK
    wide = RWKV_PAIRS * LANES
    spec = pl.BlockSpec((CHUNK, wide), lambda bi, p, ci: (bi * nchunk + ci, p))
    return pl.pallas_call(
        _rwkv_prompt_kernel,
        out_shape=(jax.ShapeDtypeStruct((batch * seq, B_WIDTH), F32),
                   jax.ShapeDtypeStruct((batch, npair, LANES, LANES), F32)),
        grid=(batch, npair // RWKV_PAIRS, nchunk),
        in_specs=[spec] * 6,
        out_specs=(spec, pl.BlockSpec((None, RWKV_PAIRS, LANES, LANES), lambda bi, p, ci: (bi, p, 0, 0))),
        scratch_shapes=[pltpu.VMEM((RWKV_PAIRS, LANES, LANES), F32)],
        compiler_params=_cparams(("parallel", "parallel", "arbitrary")),
        name="rwkv_prompt",
    )(r, lw, k, v, a, b)


def _rwkv_sample_kernel(r_ref, lw_ref, k_ref, v_ref, a_ref, b_ref, s0_ref, y_ref, s_ref, *, t_new):
    c = SAMPLE_CHUNK
    row = lax.broadcasted_iota(jnp.int32, (c, LANES), 0)
    def sel(ref, sl, e):
        x = ref[:, sl]
        if e:
            x = pltpu.roll(x, c - e * t_new, 0)
        return jnp.where(row < t_new, x, 0.0)

    ids = [(p, e) for p in range(RWKV_PAIRS) for e in range(SAMPLE_PAIR)]
    chains = []
    for p, e in ids:
        sl = slice(p * LANES, (p + 1) * LANES)
        chains.append(tuple(sel(ref, sl, e) for ref in (r_ref, lw_ref, k_ref, v_ref, a_ref, b_ref)) + (s0_ref[e, p],))
    res = dict(zip(ids, _rwkv_chunk(chains, c)))
    for p in range(RWKV_PAIRS):
        y_all = res[(p, 0)][0]
        for e in range(1, SAMPLE_PAIR):
            y_all = jnp.where(row // t_new == e, pltpu.roll(res[(p, e)][0], e * t_new, 0), y_all)
        y_ref[:, p * LANES:(p + 1) * LANES] = y_all
        for e in range(SAMPLE_PAIR):
            s_ref[e, p] = res[(p, e)][1]


def _rwkv_sample(r, lw, k, v, a, b, s0_bd, row0, dec_batch, t_new):
    npair = B_WIDTH // LANES
    nrow = SAMPLE_PAIR * t_new
    rb0 = row0 // nrow
    wide = RWKV_PAIRS * LANES
    spec = pl.BlockSpec((nrow, wide), lambda i, p: (rb0 + i, p))
    s_spec = pl.BlockSpec((SAMPLE_PAIR, RWKV_PAIRS, LANES, LANES), lambda i, p: (i, p, 0, 0))
    return pl.pallas_call(
        functools.partial(_rwkv_sample_kernel, t_new=t_new),
        out_shape=(jax.ShapeDtypeStruct((dec_batch * t_new, B_WIDTH), F32),
                   jax.ShapeDtypeStruct(s0_bd.shape, F32)),
        grid=(dec_batch // SAMPLE_PAIR, npair // RWKV_PAIRS),
        in_specs=[spec] * 6 + [s_spec],
        out_specs=(pl.BlockSpec((nrow, wide), lambda i, p: (i, p)), s_spec),
        compiler_params=_cparams(("parallel", "parallel")),
        name="rwkv_sample",
    )(r, lw, k, v, a, b, s0_bd)


def _rwkv_post_kernel(y_ref, r_ref, k_ref, v_ref, g_ref, vec_ref, o_ref):
    ones_bd = _segsum_mat()
    inv = 1.0 / HEAD_B
    for s in range(B_WIDTH // LANES):
        sl = slice(s * LANES, (s + 1) * LANES)
        y = y_ref[:, sl]
        mean = _segsum(y, ones_bd) * inv
        dlt = y - mean
        var = _segsum(dlt * dlt, ones_bd) * inv
        yn = dlt * lax.rsqrt(var + GN_EPS) * vec_ref[0:1, sl] + vec_ref[1:2, sl]
        bonus = _segsum(r_ref[:, sl] * k_ref[:, sl] * vec_ref[2:3, sl], ones_bd) * v_ref[:, sl]
        o_ref[:, sl] = ((yn + bonus) * g_ref[:, sl]).astype(o_ref.dtype)


def _rwkv_post(y, r, k_mod, v, g, vecs, tm=256):
    n = y.shape[0]
    wide = pl.BlockSpec((tm, B_WIDTH), lambda i: (i, 0))
    return pl.pallas_call(
        _rwkv_post_kernel,
        out_shape=jax.ShapeDtypeStruct((n, B_WIDTH), BF16),
        grid=(n // tm,),
        in_specs=[wide] * 5 + [pl.BlockSpec((3, B_WIDTH), lambda i: (0, 0))],
        out_specs=wide,
        compiler_params=_cparams(("parallel",)),
        name="rwkv_post",
    )(y, r, k_mod, v, g, vecs)


def _read_kernel(ao_ref, bo_ref, wa_ref, wb_ref, ga_ref, gb_ref, o_ref):
    ra = jnp.dot(ao_ref[...], wa_ref[...].astype(BF16), preferred_element_type=F32)
    rb = jnp.dot(bo_ref[...], wb_ref[...].astype(BF16), preferred_element_type=F32)
    o_ref[...] = (ga_ref[...] * ra + gb_ref[...] * rb).astype(o_ref.dtype)


def _gated_read(a_out, b_out, w_read_a, w_read_b, gates, tm, tn=512):
    n = a_out.shape[0]
    ncol = D_MODEL // tn
    return pl.pallas_call(
        _read_kernel,
        out_shape=jax.ShapeDtypeStruct((n, D_MODEL), BF16),
        grid=(ncol, n // tm),
        in_specs=[pl.BlockSpec((tm, GROUP_W), lambda j, i: (i, 0)),
                  pl.BlockSpec((tm, B_WIDTH), lambda j, i: (i, 0)),
                  pl.BlockSpec((GROUP_W, tn), lambda j, i: (0, j)),
                  pl.BlockSpec((B_WIDTH, tn), lambda j, i: (0, j)),
                  pl.BlockSpec((tm, tn), lambda j, i: (i, j)),
                  pl.BlockSpec((tm, tn), lambda j, i: (i, j + ncol))],
        out_specs=pl.BlockSpec((tm, tn), lambda j, i: (i, j)),
        compiler_params=_cparams(("parallel", "parallel")),
        name="gated_read",
    )(a_out, b_out, w_read_a, w_read_b, gates, gates)


def _residual_epilogue(acc, x_ref):
    return x_ref[...] + acc


def _pack_halves(h):
    half = h.shape[1] // 2
    top = pltpu.bitcast(h[:, :half].astype(BF16).astype(F32), jnp.uint32)
    bot = pltpu.bitcast(h[:, half:].astype(BF16).astype(F32), jnp.uint32)
    return top | (bot >> 16)


def _unpack_halves(p):
    top = pltpu.bitcast(p & jnp.uint32(0xFFFF0000), F32).astype(BF16)
    bot = pltpu.bitcast(p << 16, F32).astype(BF16)
    return top, bot


def _ffn_norm_router_kernel(u_ref, g_ref, wr_ref, hb_ref, lg_ref):
    x = u_ref[...]
    ms = jnp.mean(x * x, axis=-1, keepdims=True)
    h = x * lax.rsqrt(ms + NORM_EPS) * g_ref[...]
    hb_ref[...] = _pack_halves(h)
    lg_ref[...] = _dot3(h, wr_ref[...], _NN)


def _ffn_norm_router(u, gain, w_router, tm=256):
    n, d = u.shape
    return pl.pallas_call(
        _ffn_norm_router_kernel,
        out_shape=(jax.ShapeDtypeStruct((n, d // 2), jnp.uint32), jax.ShapeDtypeStruct((n, ROUTER_PAD), F32)),
        grid=(n // tm,),
        in_specs=[pl.BlockSpec((tm, d), lambda i: (i, 0)), pl.BlockSpec((1, d), lambda i: (0, 0)),
                  pl.BlockSpec((d, ROUTER_PAD), lambda i: (0, 0))],
        out_specs=(pl.BlockSpec((tm, d // 2), lambda i: (i, 0)), pl.BlockSpec((tm, ROUTER_PAD), lambda i: (i, 0))),
        compiler_params=_cparams(("parallel",)),
        name="ffn_norm_router",
    )(u, gain.reshape(1, d), w_router)


def _route(logits, group_b, expert_b):
    n = logits.shape[0]
    group_logits = logits[:, :N_EXPERT_GROUPS] + group_b.astype(F32)
    group = jnp.argmax(group_logits, axis=-1).astype(jnp.int32)
    p_group = jnp.take_along_axis(jax.nn.softmax(group_logits, axis=-1), group[:, None], axis=-1)
    expert_logits = (logits[:, N_EXPERT_GROUPS:N_EXPERT_GROUPS + N_EXPERTS] + expert_b.astype(F32)).reshape(
        n, N_EXPERT_GROUPS, EXPERTS_PER_GROUP)
    in_group = jnp.take_along_axis(expert_logits, group[:, None, None], axis=1)[:, 0]
    top_val, top_idx = lax.top_k(in_group, TOP_K)
    gate = p_group * jax.nn.softmax(top_val, axis=-1)
    expert_id = group[:, None] * EXPERTS_PER_GROUP + top_idx.astype(jnp.int32)
    return expert_id, gate


FFN_CHUNKS = 4


def _moe_plan(expert_id):
    n_assign = expert_id.size
    n_blocks = n_assign // ROW_BLOCK + N_EXPERTS
    n_rows = n_blocks * ROW_BLOCK
    max_items = FFN_CHUNKS * n_blocks
    i32 = jnp.int32
    e_flat = expert_id.reshape(n_assign)
    onehot = (e_flat[:, None] == jnp.arange(N_EXPERTS, dtype=i32)[None, :]).astype(i32)
    running = jnp.cumsum(onehot, axis=0)
    rank = jnp.sum(onehot * (running - 1), axis=1)
    counts = running[-1]
    nblk = (counts + ROW_BLOCK - 1) // ROW_BLOCK
    blk_start = jnp.cumsum(nblk) - nblk
    pos = (blk_start[e_flat] * ROW_BLOCK + rank).astype(i32)
    row_src = jnp.zeros((n_rows,), i32).at[pos].set(jnp.arange(n_assign, dtype=i32) // TOP_K)
    item_cnt = FFN_CHUNKS * nblk
    item_end = jnp.cumsum(item_cnt)
    item_start = item_end - item_cnt
    total = item_end[-1]
    idx = jnp.arange(max_items, dtype=i32)
    ic = jnp.minimum(idx, total - 1)
    it_e = jnp.minimum(jnp.sum((item_end[None, :] <= ic[:, None]).astype(i32), axis=1), N_EXPERTS - 1)
    within = ic - item_start[it_e]
    nb = jnp.maximum(nblk[it_e], 1)
    it_wc = within // nb
    valid = idx < total
    first = jnp.logical_and(valid, within % nb == 0)
    tail = idx - total
    it_blk = jnp.where(valid, blk_start[it_e] + within % nb, total // FFN_CHUNKS + tail // FFN_CHUNKS)
    it_c = jnp.where(valid, it_wc, tail % FFN_CHUNKS)
    return dict(pos=pos, row_src=row_src, it_e=it_e, it_wc=it_wc.astype(i32), it_c=it_c.astype(i32),
                it_blk=it_blk.astype(i32), it_valid=valid.astype(i32), it_first=first.astype(i32),
                n_rows=n_rows, max_items=max_items)


GATHER_ROWS = 128


def _dispatch_kernel(src_ref, x_hbm, o_ref, sem):
    base = pl.program_id(0) * GATHER_ROWS

    def row_copy(j, src_row):
        return pltpu.make_async_copy(x_hbm.at[pl.ds(src_row, 1)], o_ref.at[pl.ds(j, 1)], sem)

    def issue(j, carry):
        row_copy(j, src_ref[base + j]).start()
        return carry

    lax.fori_loop(0, GATHER_ROWS, issue, 0)

    def drain(j, carry):
        row_copy(j, 0).wait()
        return carry

    lax.fori_loop(0, GATHER_ROWS, drain, 0)


def _dispatch_rows(x, row_src, n_rows):
    d = x.shape[1]
    return pl.pallas_call(
        _dispatch_kernel,
        out_shape=jax.ShapeDtypeStruct((n_rows, d), x.dtype),
        grid_spec=pltpu.PrefetchScalarGridSpec(
            num_scalar_prefetch=1,
            grid=(n_rows // GATHER_ROWS,),
            in_specs=[pl.BlockSpec(memory_space=pl.ANY)],
            out_specs=pl.BlockSpec((GATHER_ROWS, d), lambda i, src: (i, 0)),
            scratch_shapes=[pltpu.SemaphoreType.DMA(())],
        ),
        compiler_params=_cparams(("arbitrary",)),
        name="moe_dispatch",
    )(row_src, x)


def _ffn_up_kernel(e_ref, wc_ref, c_ref, blk_ref, valid_ref, first_ref, x_ref, wg_ref, wu_ref, h_ref, wg_bf, wu_bf):
    i = pl.program_id(0)

    @pl.when(first_ref[i] == 1)
    def _():
        wg_bf[...] = wg_ref[...].astype(BF16)
        wu_bf[...] = wu_ref[...].astype(BF16)

    @pl.when(valid_ref[i] == 0)
    def _():
        h_ref[...] = jnp.zeros_like(h_ref)

    @pl.when(valid_ref[i] == 1)
    def _():
        top, bot = _unpack_halves(x_ref[...])
        half = top.shape[1]
        gate = (jnp.dot(top, wg_bf[:half, :], preferred_element_type=F32)
                + jnp.dot(bot, wg_bf[half:, :], preferred_element_type=F32))
        up = (jnp.dot(top, wu_bf[:half, :], preferred_element_type=F32)
              + jnp.dot(bot, wu_bf[half:, :], preferred_element_type=F32))
        h_ref[...] = (jax.nn.silu(gate) * up).astype(h_ref.dtype)


def _ffn_down_kernel(e_ref, wc_ref, c_ref, blk_ref, valid_ref, first_ref, h_ref, wd_ref, o_ref, wd_bf):
    i = pl.program_id(0)

    @pl.when(first_ref[i] == 1)
    def _():
        wd_bf[...] = wd_ref[...].astype(BF16)

    @pl.when(valid_ref[i] == 0)
    def _():
        o_ref[...] = jnp.zeros_like(o_ref)

    @pl.when(valid_ref[i] == 1)
    def _():
        o_ref[...] = jnp.dot(h_ref[...], wd_bf[...], preferred_element_type=F32)


def _expert_ffn(x_sorted, plan, w_gate, w_up, w_down):
    n_rows, dh = x_sorted.shape
    d = 2 * dh
    fc = D_EXPERT // FFN_CHUNKS
    dc = d // FFN_CHUNKS
    prefetch = (plan["it_e"], plan["it_wc"], plan["it_c"], plan["it_blk"], plan["it_valid"], plan["it_first"])
    grid = (plan["max_items"],)
    hid = pl.pallas_call(
        _ffn_up_kernel,
        out_shape=jax.ShapeDtypeStruct((n_rows, D_EXPERT), BF16),
        grid_spec=pltpu.PrefetchScalarGridSpec(
            num_scalar_prefetch=len(prefetch),
            grid=grid,
            in_specs=[pl.BlockSpec((ROW_BLOCK, dh), lambda i, e, wc, c, b, v, f: (b[i], 0)),
                      pl.BlockSpec((None, d, fc), lambda i, e, wc, c, b, v, f: (e[i], 0, wc[i])),
                      pl.BlockSpec((None, d, fc), lambda i, e, wc, c, b, v, f: (e[i], 0, wc[i]))],
            out_specs=pl.BlockSpec((ROW_BLOCK, fc), lambda i, e, wc, c, b, v, f: (b[i], c[i])),
            scratch_shapes=[pltpu.VMEM((d, fc), BF16), pltpu.VMEM((d, fc), BF16)],
        ),
        compiler_params=_cparams(("arbitrary",)),
        name="expert_ffn_up",
    )(*prefetch, x_sorted, w_gate, w_up)
    out = pl.pallas_call(
        _ffn_down_kernel,
        out_shape=jax.ShapeDtypeStruct((n_rows, d), F32),
        grid_spec=pltpu.PrefetchScalarGridSpec(
            num_scalar_prefetch=len(prefetch),
            grid=grid,
            in_specs=[pl.BlockSpec((ROW_BLOCK, D_EXPERT), lambda i, e, wc, c, b, v, f: (b[i], 0)),
                      pl.BlockSpec((None, D_EXPERT, dc), lambda i, e, wc, c, b, v, f: (e[i], 0, wc[i]))],
            out_specs=pl.BlockSpec((ROW_BLOCK, dc), lambda i, e, wc, c, b, v, f: (b[i], c[i])),
            scratch_shapes=[pltpu.VMEM((D_EXPERT, dc), BF16)],
        ),
        compiler_params=_cparams(("arbitrary",)),
        name="expert_ffn_down",
    )(*prefetch, hid, w_down)
    return out


COMBINE_ROWS = 64


def _combine_kernel(pos_ref, u_ref, g_ref, y_hbm, o_ref, buf, sem):
    base = pl.program_id(0) * COMBINE_ROWS

    def issue(j, carry):
        for s in range(TOP_K):
            pltpu.make_async_copy(y_hbm.at[pl.ds(pos_ref[(base + j) * TOP_K + s], 1)],
                                  buf.at[s, pl.ds(j, 1)], sem).start()
        return carry

    lax.fori_loop(0, COMBINE_ROWS, issue, 0)

    def drain(j, carry):
        for s in range(TOP_K):
            pltpu.make_async_copy(y_hbm.at[pl.ds(0, 1)], buf.at[s, pl.ds(j, 1)], sem).wait()
        return carry

    lax.fori_loop(0, COMBINE_ROWS, drain, 0)
    g = g_ref[...]
    moe = buf[0] * g[:, 0:1] + buf[1] * g[:, 1:2]
    o_ref[...] = u_ref[...] + moe


def _combine(u, gate, y_sorted, pos):
    n, d = u.shape
    return pl.pallas_call(
        _combine_kernel,
        out_shape=jax.ShapeDtypeStruct((n, d), F32),
        grid_spec=pltpu.PrefetchScalarGridSpec(
            num_scalar_prefetch=1,
            grid=(n // COMBINE_ROWS,),
            in_specs=[pl.BlockSpec((COMBINE_ROWS, d), lambda i, p: (i, 0)),
                      pl.BlockSpec((COMBINE_ROWS, TOP_K), lambda i, p: (i, 0)),
                      pl.BlockSpec(memory_space=pl.ANY)],
            out_specs=pl.BlockSpec((COMBINE_ROWS, d), lambda i, p: (i, 0)),
            scratch_shapes=[pltpu.VMEM((TOP_K, COMBINE_ROWS, d), F32), pltpu.SemaphoreType.DMA(())],
        ),
        compiler_params=_cparams(("arbitrary",)),
        name="moe_combine",
    )(pos, u, gate, y_sorted)


def _pick_tm(n):
    for tm in (1088, 1024, 544, 512, 272, 256, 128):
        if n % tm == 0:
            return tm
    raise ValueError(f"row count {n} is not a multiple of 128")


def kernel(x_prompt, x_sample, cache_kv_w128, cache_kv_w512, cache_kv_w2048, state_shift, state_wkv, norm_mix, w_in, q_norm, k_norm, mu_shift, rwkv_w0, rwkv_w2, rwkv_a0, rwkv_a2, rwkv_g2, rwkv_k_k, rwkv_k_a, rwkv_r_k, rwkv_ln_w, rwkv_ln_b, w_read_a, w_read_b, w_o, norm_ffn, router_group_w, router_group_b, router_expert_w, router_expert_b, expert_w_gate, expert_w_up, expert_w_down):
    batch, seq, d = x_prompt.shape
    dec_batch, t_new, _ = x_sample.shape
    n_p, n_s = batch * seq, dec_batch * t_new
    n = n_p + n_s
    past = cache_kv_w2048.shape[1]
    assert d == D_MODEL and seq % (DILATIONS[2] * ROW_BLOCK) == 0 and seq % CHUNK == 0
    assert cache_kv_w128.shape[1] == 128 and cache_kv_w512.shape[1] == 512 and past == 2048
    tm = _pick_tm(n)

    x = jnp.concatenate([x_prompt.reshape(n_p, d), x_sample.reshape(n_s, d)], axis=0)
    h = _rmsnorm(x, norm_mix)
    h_p = h[:n_p].reshape(batch, seq, d)
    h_s = h[n_p:].reshape(dec_batch, t_new, d)
    prev = jnp.concatenate([
        jnp.concatenate([jnp.zeros((batch, 1, d), F32), h_p[:, :-1]], axis=1).reshape(n_p, d),
        jnp.concatenate([state_shift[:, None].astype(F32), h_s[:, :-1]], axis=1).reshape(n_s, d)], axis=0)
    hb, mixes = _token_shift_mix(h, prev, mu_shift)

    positions = jnp.concatenate([jnp.tile(jnp.arange(seq, dtype=jnp.int32), batch),
                                 jnp.tile(past + jnp.arange(t_new, dtype=jnp.int32), dec_batch)])
    rope_c, rope_a, rope_b = _rope_tables(positions)
    gains = jnp.stack([q_norm, k_norm]).reshape(2, 1, HEAD_A).astype(F32)
    tn = GROUP_W
    row_spec = pl.BlockSpec((tm, HEAD_A), lambda j, i: (i, 0))
    qk = _matmul(hb, w_in, col_off=0, width=2 * A_WIDTH, tm=tm, tn=tn, epilogue=_qk_epilogue,
                 extra=(gains, rope_c, rope_a, rope_b),
                 extra_specs=(pl.BlockSpec((None, 1, HEAD_A), lambda j, i: (j // N_GROUPS, 0, 0)),
                              row_spec, row_spec, row_spec), name="proj_qk")
    v_a = _matmul(hb, w_in, col_off=2 * A_WIDTH, width=A_WIDTH, tm=tm, tn=tn, name="proj_v")
    gates = _matmul(hb, w_in, col_off=3 * A_WIDTH, width=2 * D_MODEL, tm=tm, tn=tn,
                    epilogue=_sigmoid_epilogue, name="proj_gates")
    r_b = _matmul(mixes, w_in, a_sel=0, col_off=X_COLS, width=B_WIDTH, tm=tm, tn=tn, name="proj_r")
    k_b = _matmul(mixes, w_in, a_sel=1, col_off=X_COLS + B_WIDTH, width=B_WIDTH, tm=tm, tn=tn, name="proj_k")
    v_b = _matmul(mixes, w_in, a_sel=2, col_off=X_COLS + 2 * B_WIDTH, width=B_WIDTH, tm=tm, tn=tn, name="proj_vb")
    lora0 = X_COLS + 3 * B_WIDTH
    w_lora = jnp.pad(w_in[:, lora0:], ((0, 0), (0, GATE_LORA_PAD - GATE_LORA)))
    w_low = _matmul(mixes, w_lora, a_sel=3, col_off=0, width=DECAY_LORA, tm=tm, tn=DECAY_LORA, name="proj_wlow")
    a_low = _matmul(mixes, w_lora, a_sel=4, col_off=DECAY_LORA, width=AAA_LORA, tm=tm, tn=AAA_LORA, name="proj_alow")
    g_low = _matmul(mixes, w_lora, a_sel=5, col_off=DECAY_LORA + AAA_LORA, width=GATE_LORA_PAD, tm=tm,
                    tn=DECAY_LORA, name="proj_glow")

    outs, lses = [], []
    for g in range(N_GROUPS):
        o_g, l_g = _prompt_group_attention(qk, v_a, g, batch, seq)
        outs.append(o_g)
        lses.append(l_g)
    a_out_p = _merge_groups(outs, lses)
    caches = (cache_kv_w128, cache_kv_w512, cache_kv_w2048)
    a_out_s = _sample_attention(qk, v_a, caches, n_p, dec_batch, t_new)
    a_out = jnp.concatenate([a_out_p, a_out_s], axis=0)

    def kv_rows(g, rows):
        kk = qk[rows, A_WIDTH + g * GROUP_W:A_WIDTH + (g + 1) * GROUP_W]
        vv = v_a[rows, g * GROUP_W:(g + 1) * GROUP_W]
        return jnp.stack([kk.reshape(-1, HEADS_PER_GROUP, HEAD_A), vv.reshape(-1, HEADS_PER_GROUP, HEAD_A)], axis=1)

    kv_prompt = []
    for g in range(N_GROUPS):
        keep = min(caches[g].shape[1], seq)
        full = kv_rows(g, slice(0, n_p)).reshape(batch, seq, 2, HEADS_PER_GROUP, HEAD_A)
        kv_prompt.append(full[:, seq - keep:])
    news = [kv_rows(g, slice(n_p, n)).reshape(dec_batch, t_new, 2, HEADS_PER_GROUP, HEAD_A) for g in range(N_GROUPS)]
    kv_sample = _roll_caches(caches, news, t_new)

    g2_pad = jnp.pad(rwkv_g2, ((0, GATE_LORA_PAD - GATE_LORA), (0, 0)))
    vec_prep = jnp.stack([rwkv_w0, rwkv_a0, rwkv_k_k, rwkv_k_a]).astype(F32)
    lw, k_mod, neg_kk, kk_a, g_out = _rwkv_prep(k_b, w_low, a_low, g_low, rwkv_w2, rwkv_a2, g2_pad, vec_prep)
    y_p, s_p = _rwkv_prompt(r_b, lw, k_mod, v_b, neg_kk, kk_a, batch, seq)
    npair = B_WIDTH // LANES
    s0 = state_wkv.astype(F32).reshape(dec_batch, npair, 2, HEAD_B, HEAD_B)
    zblk = jnp.zeros_like(s0[:, :, 0])
    s0_bd = jnp.concatenate([jnp.concatenate([s0[:, :, 0], zblk], axis=-1),
                             jnp.concatenate([zblk, s0[:, :, 1]], axis=-1)], axis=-2)
    y_s, s_s = _rwkv_sample(r_b, lw, k_mod, v_b, neg_kk, kk_a, s0_bd, n_p, dec_batch, t_new)

    def unpair(s_bd, nb):
        top = s_bd[:, :, :HEAD_B, :HEAD_B]
        bot = s_bd[:, :, HEAD_B:, HEAD_B:]
        return jnp.stack([top, bot], axis=2).reshape(nb, 2 * npair, HEAD_B, HEAD_B)

    y_b = jnp.concatenate([y_p, y_s], axis=0)
    vec_post = jnp.stack([rwkv_ln_w, rwkv_ln_b, rwkv_r_k.reshape(B_WIDTH)]).astype(F32)
    b_out = _rwkv_post(y_b, r_b, k_mod, v_b, g_out, vec_post)

    merged = _gated_read(a_out, b_out, w_read_a, w_read_b, gates, tm)
    u = _matmul(merged, w_o, tm=tm, tn=tn, epilogue=_residual_epilogue, extra=(x,),
                extra_specs=(pl.BlockSpec((tm, tn), lambda j, i: (i, j)),), name="proj_out")
    w_router = jnp.pad(jnp.concatenate([router_group_w, router_expert_w], axis=1),
                       ((0, 0), (0, ROUTER_PAD - N_EXPERT_GROUPS - N_EXPERTS)))
    hn, logits = _ffn_norm_router(u, norm_ffn, w_router)
    expert_id, gate = _route(logits, router_group_b, router_expert_b)
    plan = _moe_plan(expert_id)
    x_sorted = _dispatch_rows(hn, plan["row_src"], plan["n_rows"])
    y_sorted = _expert_ffn(x_sorted, plan, expert_w_gate, expert_w_up, expert_w_down)
    y = _combine(u, gate, y_sorted, plan["pos"])

    return (y[:n_p].reshape(batch, seq, d), y[n_p:].reshape(dec_batch, t_new, d),
            kv_prompt[0], kv_prompt[1], kv_prompt[2], h_p[:, -1], unpair(s_p, batch).astype(state_wkv.dtype),
            kv_sample[0], kv_sample[1], kv_sample[2], h_s[:, -1], unpair(s_s, dec_batch).astype(state_wkv.dtype))
```

```python
import functools

import jax
import jax.numpy as jnp
from jax import lax
from jax.experimental import pallas as pl
from jax.experimental.pallas import tpu as pltpu

F32 = jnp.float32
BF16 = jnp.bfloat16

LANES = 128
VMEM_LIMIT = 48 * 1024 * 1024

D_MODEL = 4096
HEAD_A = 128
N_GROUPS = 3
HEADS_PER_GROUP = 4
GROUP_W = HEADS_PER_GROUP * HEAD_A
A_WIDTH = N_GROUPS * GROUP_W
DILATIONS = (1, 4, 16)
WINDOW_KEYS = 128
ROPE_THETA = 500000.0
ROPE_DIM = HEAD_A // 4
ROPE_HALF = ROPE_DIM // 2
HEAD_B = 64
B_WIDTH = D_MODEL // 2
DECAY_LORA = 128
AAA_LORA = 128
GATE_LORA = 480
GATE_LORA_PAD = 512
GN_EPS = 64e-5
NORM_EPS = 1e-6
X_COLS = 3 * A_WIDTH + 2 * D_MODEL
N_EXPERT_GROUPS = 8
EXPERTS_PER_GROUP = 8
N_EXPERTS = 64
TOP_K = 2
D_EXPERT = 1024
ROW_BLOCK = 128
ROUTER_PAD = 128
CHUNK = 64
SAMPLE_CHUNK = 8


def _cparams(sem, vmem=VMEM_LIMIT):
    return pltpu.CompilerParams(dimension_semantics=sem, vmem_limit_bytes=vmem)


def _rmsnorm_kernel(x_ref, g_ref, o_ref):
    x = x_ref[...]
    ms = jnp.mean(x * x, axis=-1, keepdims=True)
    o_ref[...] = x * lax.rsqrt(ms + NORM_EPS) * g_ref[...]


def _rmsnorm(x, gain, tm=256):
    n, d = x.shape
    return pl.pallas_call(
        _rmsnorm_kernel,
        out_shape=jax.ShapeDtypeStruct((n, d), F32),
        grid=(n // tm,),
        in_specs=[pl.BlockSpec((tm, d), lambda i: (i, 0)), pl.BlockSpec((1, d), lambda i: (0, 0))],
        out_specs=pl.BlockSpec((tm, d), lambda i: (i, 0)),
        compiler_params=_cparams(("parallel",)),
        name="rmsnorm",
    )(x, gain.reshape(1, d))


SUBLANES = 8


def _mix_kernel(h_ref, before_ref, first_ref, mu_ref, hb_ref, m_ref, *, n_prompt, seq, t_new):
    i = pl.program_id(0)
    tm = h_ref.shape[0]
    h = h_ref[...]
    row = lax.broadcasted_iota(jnp.int32, (tm, 1), 0)
    prev = jnp.where(row == 0, before_ref[SUBLANES - 1:SUBLANES, :], pltpu.roll(h, 1, 0))
    is_sample = i * tm >= n_prompt
    sample_start = jnp.where(row % t_new == 0, 1, 0)
    prompt_start = jnp.where((i * tm + row) % seq == 0, 1, 0)
    starts = jnp.where(is_sample, sample_start, prompt_start) == 1
    first = jnp.where(is_sample, first_ref[...], 0.0)
    xx = jnp.where(starts, first, prev) - h
    hb_ref[...] = h.astype(BF16)
    for j in range(6):
        m_ref[j] = (h + xx * mu_ref[j:j + 1, :]).astype(BF16)


def _token_shift_mix(h, first_rows, mu, n_prompt, seq, t_new, tm=128):
    n, d = h.shape
    assert n_prompt % tm == 0 and seq % tm == 0 and tm % t_new == 0 and tm % SUBLANES == 0
    per = tm // SUBLANES
    kern = functools.partial(_mix_kernel, n_prompt=n_prompt, seq=seq, t_new=t_new)
    return pl.pallas_call(
        kern,
        out_shape=(jax.ShapeDtypeStruct((n, d), BF16), jax.ShapeDtypeStruct((6, n, d), BF16)),
        grid=(n // tm,),
        in_specs=[pl.BlockSpec((tm, d), lambda i: (i, 0)),
                  pl.BlockSpec((SUBLANES, d), lambda i: (jnp.maximum(i * per - 1, 0), 0)),
                  pl.BlockSpec((tm, d), lambda i: (jnp.maximum(i - n_prompt // tm, 0), 0)),
                  pl.BlockSpec((6, d), lambda i: (0, 0))],
        out_specs=(pl.BlockSpec((tm, d), lambda i: (i, 0)), pl.BlockSpec((6, tm, d), lambda i: (0, i, 0))),
        compiler_params=_cparams(("parallel",)),
        name="token_shift_mix",
    )(h, h, first_rows, mu)


def _mm_kernel(*refs, n_extra, epilogue, w_transposed):
    a_ref, w_ref = refs[0], refs[1]
    extra = refs[2:2 + n_extra]
    o_ref = refs[2 + n_extra]
    dims = _NT if w_transposed else _NN
    acc = lax.dot_general(a_ref[...], w_ref[...].astype(BF16), (dims, ((), ())), preferred_element_type=F32)
    o_ref[...] = epilogue(acc, *extra).astype(o_ref.dtype)


def _matmul(a, w, *, col_off=0, width=None, tm, tn, epilogue=None, extra=(), extra_specs=(),
            out_dtype=F32, a_sel=None, w_transposed=False, name="matmul"):
    if a_sel is None:
        n, k = a.shape
        a_spec = pl.BlockSpec((tm, k), lambda j, i: (i, 0))
    else:
        _, n, k = a.shape
        a_spec = pl.BlockSpec((None, tm, k), lambda j, i: (a_sel, i, 0))
    n_cols = w.shape[0] if w_transposed else w.shape[1]
    width = n_cols - col_off if width is None else width
    assert col_off % tn == 0 and width % tn == 0 and n % tm == 0
    off = col_off // tn
    if epilogue is None:
        epilogue = lambda acc: acc
    kern = functools.partial(_mm_kernel, n_extra=len(extra), epilogue=epilogue, w_transposed=w_transposed)
    w_spec = (pl.BlockSpec((tn, k), lambda j, i: (j + off, 0)) if w_transposed
              else pl.BlockSpec((k, tn), lambda j, i: (0, j + off)))
    return pl.pallas_call(
        kern,
        out_shape=jax.ShapeDtypeStruct((n, width), out_dtype),
        grid=(width // tn, n // tm),
        in_specs=[a_spec, w_spec] + list(extra_specs),
        out_specs=pl.BlockSpec((tm, tn), lambda j, i: (i, j)),
        compiler_params=_cparams(("parallel", "parallel")),
        name=name,
    )(a, w, *extra)


def _qk_epilogue(acc, gain_ref, c_ref, a_ref, b_ref):
    c, a, b = c_ref[...], a_ref[...], b_ref[...]
    g = gain_ref[...]
    outs = []
    for hh in range(HEADS_PER_GROUP):
        x = acc[:, hh * HEAD_A:(hh + 1) * HEAD_A]
        ms = jnp.mean(x * x, axis=-1, keepdims=True)
        y = x * lax.rsqrt(ms + NORM_EPS) * g
        y = y * c + pltpu.roll(y, HEAD_A - ROPE_HALF, 1) * a + pltpu.roll(y, ROPE_HALF, 1) * b
        outs.append(y)
    return jnp.concatenate(outs, axis=1)


def _sigmoid_epilogue(acc):
    return jax.nn.sigmoid(acc)


def _valid_cols_epilogue(acc, *, valid):
    col = pl.program_id(0) * acc.shape[1] + lax.broadcasted_iota(jnp.int32, acc.shape, 1)
    return jnp.where(col < valid, acc, 0.0)


def _rope_tables(positions):
    n = positions.shape[0]
    inv_freq = ROPE_THETA ** (-jnp.arange(ROPE_HALF, dtype=F32) / ROPE_HALF)
    ang = positions.astype(F32)[:, None] * inv_freq[None, :]
    cos, sin = jnp.cos(ang), jnp.sin(ang)
    zeros = lambda w: jnp.zeros((n, w), F32)
    c = jnp.concatenate([cos, cos, jnp.ones((n, HEAD_A - ROPE_DIM), F32)], axis=1)
    a = jnp.concatenate([-sin, zeros(HEAD_A - ROPE_HALF)], axis=1)
    b = jnp.concatenate([zeros(ROPE_HALF), sin, zeros(HEAD_A - ROPE_DIM)], axis=1)
    return c, a, b


def _softmax_parts(s):
    m = jnp.max(s, axis=-1, keepdims=True)
    e = jnp.exp(s - m)
    denom = jnp.sum(e, axis=-1, keepdims=True)
    return e / denom, m + jnp.log(denom)


def _prompt_attn_kernel(q_ref, kp_ref, kc_ref, vp_ref, vc_ref, o_ref, l_ref):
    qb = pl.program_id(2)
    rows = lax.broadcasted_iota(jnp.int32, (ROW_BLOCK, ROW_BLOCK), 0)
    cols = lax.broadcasted_iota(jnp.int32, (ROW_BLOCK, ROW_BLOCK), 1)
    prev_ok = jnp.logical_and(cols >= rows, qb > 0)
    cur_ok = cols <= rows
    mask = jnp.concatenate([prev_ok, cur_ok], axis=1)
    scale = HEAD_A ** -0.5
    for hh in range(HEADS_PER_GROUP):
        sl = slice(hh * HEAD_A, (hh + 1) * HEAD_A)
        q = q_ref[:, sl].astype(BF16)
        k = jnp.concatenate([kp_ref[:, sl], kc_ref[:, sl]], axis=0).astype(BF16)
        v = jnp.concatenate([vp_ref[:, sl], vc_ref[:, sl]], axis=0).astype(BF16)
        s = lax.dot_general(q, k, (((1,), (1,)), ((), ())), preferred_element_type=F32) * scale
        s = jnp.where(mask, s, -jnp.inf)
        p, lse = _softmax_parts(s)
        o_ref[:, sl] = jnp.dot(p.astype(BF16), v, preferred_element_type=F32)
        l_ref[:, sl] = jnp.broadcast_to(lse, (ROW_BLOCK, HEAD_A))


def _prompt_group_attention(qk, v, g, batch, seq):
    d = DILATIONS[g]
    n = qk.shape[0]
    lc = seq // d
    nqb = lc // ROW_BLOCK
    qk_v = qk.reshape(n // d, d * 2 * A_WIDTH)
    v_v = v.reshape(n // d, d * A_WIDTH)
    qcols = 2 * A_WIDTH // GROUP_W
    vcols = A_WIDTH // GROUP_W

    def row(b, qb):
        return b * nqb + qb

    q_spec = pl.BlockSpec((ROW_BLOCK, GROUP_W), lambda b, c, qb: (row(b, qb), c * qcols + g))
    kp_spec = pl.BlockSpec((ROW_BLOCK, GROUP_W), lambda b, c, qb: (row(b, jnp.maximum(qb - 1, 0)), c * qcols + N_GROUPS + g))
    kc_spec = pl.BlockSpec((ROW_BLOCK, GROUP_W), lambda b, c, qb: (row(b, qb), c * qcols + N_GROUPS + g))
    vp_spec = pl.BlockSpec((ROW_BLOCK, GROUP_W), lambda b, c, qb: (row(b, jnp.maximum(qb - 1, 0)), c * vcols + g))
    vc_spec = pl.BlockSpec((ROW_BLOCK, GROUP_W), lambda b, c, qb: (row(b, qb), c * vcols + g))
    o_spec = pl.BlockSpec((ROW_BLOCK, GROUP_W), lambda b, c, qb: (row(b, qb), c))
    out_sds = jax.ShapeDtypeStruct((batch * seq // d, d * GROUP_W), F32)
    o, lse = pl.pallas_call(
        _prompt_attn_kernel,
        out_shape=(out_sds, out_sds),
        grid=(batch, d, nqb),
        in_specs=[q_spec, kp_spec, kc_spec, vp_spec, vc_spec],
        out_specs=(o_spec, o_spec),
        compiler_params=_cparams(("parallel", "parallel", "parallel")),
        name=f"prompt_attn_g{g}",
    )(qk_v, qk_v, qk_v, v_v, v_v)
    return o.reshape(batch * seq, GROUP_W), lse.reshape(batch * seq, GROUP_W)


def _merge_kernel(o0, o1, o2, l0, l1, l2, out_ref):
    ls = [l0[...], l1[...], l2[...]]
    m = jnp.maximum(jnp.maximum(ls[0], ls[1]), ls[2])
    es = [jnp.exp(l - m) for l in ls]
    tot = es[0] + es[1] + es[2]
    acc = (es[0] / tot) * o0[...] + (es[1] / tot) * o1[...] + (es[2] / tot) * o2[...]
    out_ref[...] = acc.astype(out_ref.dtype)


def _merge_groups(outs, lses, tm=512):
    n = outs[0].shape[0]
    spec = pl.BlockSpec((tm, GROUP_W), lambda i: (i, 0))
    return pl.pallas_call(
        _merge_kernel,
        out_shape=jax.ShapeDtypeStruct((n, GROUP_W), BF16),
        grid=(n // tm,),
        in_specs=[spec] * 6,
        out_specs=spec,
        compiler_params=_cparams(("parallel",)),
        name="merge_groups",
    )(*outs, *lses)


SAMPLE_PAIR = 2
NEG_BIG = -1e30
SAMPLE_ATTN_VMEM = 56 * 1024 * 1024


def _sample_attn_kernel(q_ref, k_ref, v_ref, c0_ref, c1_ref, c2_ref, out_ref, *, t_new):
    nrow = SAMPLE_PAIR * t_new
    scale = HEAD_A ** -0.5
    qrow = lax.broadcasted_iota(jnp.int32, (nrow, 1), 0)
    q_elem, q_t = qrow // t_new, qrow % t_new
    res = [None] * HEADS_PER_GROUP
    for hh in range(HEADS_PER_GROUP):
        outs, lses = [], []
        for g in range(N_GROUPS):
            col = (g * HEADS_PER_GROUP + hh) * HEAD_A
            q = q_ref[:, col:col + HEAD_A].astype(BF16)
            k_new = k_ref[:, col:col + HEAD_A]
            v_new = v_ref[:, col:col + HEAD_A]
            o_g, l_g = None, None
            for e in range(SAMPLE_PAIR):
                if g == 0:
                    k_c = c0_ref[e, :, 0, hh, :]
                    v_c = c0_ref[e, :, 1, hh, :]
                    ncache = WINDOW_KEYS
                else:
                    cref = c1_ref if g == 1 else c2_ref
                    k_c = jnp.concatenate([cref[e, :, c, 0, hh, :] for c in range(t_new)], axis=0)
                    v_c = jnp.concatenate([cref[e, :, c, 1, hh, :] for c in range(t_new)], axis=0)
                    ncache = t_new * WINDOW_KEYS
                k_all = jnp.concatenate([k_c, k_new], axis=0).astype(BF16)
                v_all = jnp.concatenate([v_c, v_new], axis=0).astype(BF16)
                s = lax.dot_general(q, k_all, (((1,), (1,)), ((), ())), preferred_element_type=F32) * scale
                kcol = lax.broadcasted_iota(jnp.int32, (1, ncache + nrow), 1)
                is_new = kcol >= ncache
                new_idx = kcol - ncache
                if g == 0:
                    ok_cache = kcol >= q_t
                    ok_new = jnp.logical_and(new_idx // t_new == e, new_idx % t_new <= q_t)
                else:
                    ok_cache = kcol // WINDOW_KEYS == q_t
                    ok_new = jnp.logical_and(new_idx // t_new == e, new_idx % t_new == q_t)
                ok = jnp.logical_or(jnp.logical_and(is_new, ok_new),
                                    jnp.logical_and(jnp.logical_not(is_new), ok_cache))
                ok = jnp.logical_and(ok, q_elem == e)
                s = jnp.where(ok, s, NEG_BIG)
                p, lse = _softmax_parts(s)
                o_e = jnp.dot(p.astype(BF16), v_all, preferred_element_type=F32)
                if e == 0:
                    o_g, l_g = o_e, lse
                else:
                    o_g = jnp.where(q_elem == e, o_e, o_g)
                    l_g = jnp.where(q_elem == e, lse, l_g)
            outs.append(o_g)
            lses.append(l_g)
        m = jnp.maximum(jnp.maximum(lses[0], lses[1]), lses[2])
        es = [jnp.exp(l - m) for l in lses]
        tot = es[0] + es[1] + es[2]
        res[hh] = (es[0] / tot) * outs[0] + (es[1] / tot) * outs[1] + (es[2] / tot) * outs[2]
    out_ref[...] = jnp.concatenate(res, axis=1).astype(out_ref.dtype)


def _sample_attention(qk, v, caches, row0, dec_batch, t_new):
    n = qk.shape[0]
    nrow = SAMPLE_PAIR * t_new
    assert nrow == 8 and row0 % nrow == 0 and dec_batch % SAMPLE_PAIR == 0
    rb0 = row0 // nrow
    tail = caches[0].shape[2:]
    c1 = caches[1].reshape((dec_batch, WINDOW_KEYS, DILATIONS[1]) + tail)
    c2 = caches[2].reshape((dec_batch, WINDOW_KEYS, DILATIONS[2]) + tail)
    assert DILATIONS[1] == t_new
    kern = functools.partial(_sample_attn_kernel, t_new=t_new)
    class_blk = (SAMPLE_PAIR, WINDOW_KEYS, t_new) + tail
    return pl.pallas_call(
        kern,
        out_shape=jax.ShapeDtypeStruct((dec_batch * t_new, GROUP_W), BF16),
        grid=(dec_batch // SAMPLE_PAIR,),
        in_specs=[pl.BlockSpec((nrow, A_WIDTH), lambda i: (rb0 + i, 0)),
                  pl.BlockSpec((nrow, A_WIDTH), lambda i: (rb0 + i, 1)),
                  pl.BlockSpec((nrow, A_WIDTH), lambda i: (rb0 + i, 0)),
                  pl.BlockSpec((SAMPLE_PAIR, WINDOW_KEYS) + tail, lambda i: (i, 0, 0, 0, 0)),
                  pl.BlockSpec(class_blk, lambda i: (i, 0, 0, 0, 0, 0)),
                  pl.BlockSpec(class_blk, lambda i: (i, 0, 0, 0, 0, 0))],
        out_specs=pl.BlockSpec((nrow, GROUP_W), lambda i: (i, 0)),
        compiler_params=_cparams(("parallel",), vmem=SAMPLE_ATTN_VMEM),
        name="sample_attn",
    )(qk, qk, v, caches[0], c1, c2)


def _cache_roll_kernel(c_ref, n_ref, o_ref):
    last = pl.program_id(0) == pl.num_programs(0) - 1
    o_ref[...] = jnp.where(last, n_ref[...], c_ref[...])


def _roll_cache(cache, new, t_new):
    nb, length = cache.shape[0], cache.shape[1]
    assert length % t_new == 0
    nblk = length // t_new
    blk = (nb, t_new) + cache.shape[2:]
    return pl.pallas_call(
        _cache_roll_kernel,
        out_shape=jax.ShapeDtypeStruct(cache.shape, cache.dtype),
        grid=(nblk,),
        in_specs=[pl.BlockSpec(blk, lambda j: (0, jnp.minimum(j + 1, nblk - 1), 0, 0, 0)),
                  pl.BlockSpec(blk, lambda j: (0, 0, 0, 0, 0))],
        out_specs=pl.BlockSpec(blk, lambda j: (0, j, 0, 0, 0)),
        compiler_params=_cparams(("parallel",)),
        name=f"cache_roll_w{length}",
    )(cache, new)


def _roll_caches(caches, news, t_new):
    return [_roll_cache(c, x.astype(c.dtype), t_new) for c, x in zip(caches, news)]


def _split_bf16(x):
    hi = x.astype(BF16)
    lo = (x - hi.astype(F32)).astype(BF16)
    return hi, lo


def _dot3(a, b, dims):
    ah, al = _split_bf16(a)
    bh, bl = _split_bf16(b)
    dn = (dims, ((), ()))
    out = lax.dot_general(ah, bh, dn, preferred_element_type=F32)
    out = out + lax.dot_general(ah, bl, dn, preferred_element_type=F32)
    return out + lax.dot_general(al, bh, dn, preferred_element_type=F32)


def _cumsum_rows(tri, x):
    hi = x.astype(BF16)
    rem = x - hi.astype(F32)
    mid = rem.astype(BF16)
    lo = (rem - mid.astype(F32)).astype(BF16)
    return (jnp.dot(tri, hi, preferred_element_type=F32) + jnp.dot(tri, mid, preferred_element_type=F32)
            + jnp.dot(tri, lo, preferred_element_type=F32))


_NN = ((1,), (0,))
_NT = ((1,), (1,))
_TN = ((0,), (0,))


def _segsum_mat():
    r = lax.broadcasted_iota(jnp.int32, (LANES, LANES), 0) // HEAD_B
    c = lax.broadcasted_iota(jnp.int32, (LANES, LANES), 1) // HEAD_B
    return (r == c).astype(BF16)


def _segsum(x, ones_bd):
    hi, lo = _split_bf16(x)
    return (jnp.dot(hi, ones_bd, preferred_element_type=F32) + jnp.dot(lo, ones_bd, preferred_element_type=F32))


def _rwkv_prep_kernel(k_ref, wl_ref, al_ref, gl_ref, w2_ref, a2_ref, g2_ref, vec_ref,
                      lw_ref, km_ref, na_ref, nb_ref, g_ref):
    w0, a0, k_k, k_a = vec_ref[0:1, :], vec_ref[1:2, :], vec_ref[2:3, :], vec_ref[3:4, :]
    w_lin = w0 + jnp.dot(jnp.tanh(wl_ref[...]).astype(BF16), w2_ref[...].astype(BF16), preferred_element_type=F32)
    w_log = -jax.nn.softplus(-w_lin) - 0.5
    lw_ref[...] = -jnp.exp(w_log)
    a = jax.nn.sigmoid(a0 + jnp.dot(al_ref[...].astype(BF16), a2_ref[...].astype(BF16), preferred_element_type=F32))
    g_ref[...] = jnp.dot(jax.nn.sigmoid(gl_ref[...]).astype(BF16), g2_ref[...].astype(BF16), preferred_element_type=F32)
    k = k_ref[...]
    km_ref[...] = k * (1.0 + (a - 1.0) * k_a)
    kk = k * k_k
    ones_bd = _segsum_mat()
    for s in range(B_WIDTH // LANES):
        sl = slice(s * LANES, (s + 1) * LANES)
        kks = kk[:, sl]
        nrm = jnp.maximum(jnp.sqrt(_segsum(kks * kks, ones_bd)), 1e-12)
        kkn = kks / nrm
        na_ref[:, sl] = -kkn
        nb_ref[:, sl] = kkn * a[:, sl]


def _rwkv_prep(k, w_low, a_low, g_low, w2, a2, g2_pad, vecs, tm=256):
    n = k.shape[0]
    wide = pl.BlockSpec((tm, B_WIDTH), lambda i: (i, 0))
    full = lambda shape: pl.BlockSpec(shape, lambda i: (0, 0))
    sds = jax.ShapeDtypeStruct((n, B_WIDTH), F32)
    return pl.pallas_call(
        _rwkv_prep_kernel,
        out_shape=(sds,) * 5,
        grid=(n // tm,),
        in_specs=[wide, pl.BlockSpec((tm, DECAY_LORA), lambda i: (i, 0)), pl.BlockSpec((tm, AAA_LORA), lambda i: (i, 0)),
                  pl.BlockSpec((tm, GATE_LORA_PAD), lambda i: (i, 0)),
                  full((DECAY_LORA, B_WIDTH)), full((AAA_LORA, B_WIDTH)), full((GATE_LORA_PAD, B_WIDTH)),
                  full((4, B_WIDTH))],
        out_specs=(wide,) * 5,
        compiler_params=_cparams(("parallel",)),
        name="rwkv_prep",
    )(k, w_low, a_low, g_low, w2, a2, g2_pad, vecs)


def _each(fn, *lists):
    return [fn(*args) for args in zip(*lists)]


def _dot1(a, b, dims):
    return lax.dot_general(a.astype(BF16), b.astype(BF16), (dims, ((), ())), preferred_element_type=F32)


def _mm_each(xs, ys, dims):
    return _each(lambda x, y: _dot1(x, y, dims), xs, ys)


def _tri_inverse(a_bds, c):
    n = 2 * c
    r = lax.broadcasted_iota(jnp.int32, (n, n), 0)
    q = lax.broadcasted_iota(jnp.int32, (n, n), 1)
    eye = (r == q).astype(F32)
    base = 8
    in_base = r // base == q // base
    a0 = [jnp.where(in_base, a, 0.0) for a in a_bds]
    a2 = _mm_each(a0, a0, _NN)
    a4 = _mm_each(a2, a2, _NN)
    x = [eye + a for a in a0]
    x = _each(jnp.add, x, _mm_each(x, a2, _NN))
    x = _each(jnp.add, x, _mm_each(x, a4, _NN))
    size = base
    while size < c:
        below = jnp.logical_and(r // (2 * size) == q // (2 * size), r // size == q // size + 1)
        off = [jnp.where(below, a, 0.0) for a in a_bds]
        x = _each(jnp.add, x, _mm_each(_mm_each(x, off, _NN), x, _NN))
        size *= 2
    return x


def _rwkv_chunk(chains, c):
    n = 2 * c
    lane = lax.broadcasted_iota(jnp.int32, (c, LANES), 1)
    m0 = (lane < HEAD_B).astype(F32)
    m1 = 1.0 - m0
    stack = lambda x: jnp.concatenate([x * m0, x * m1], axis=0)
    r, lw, k, v, a, b, s_bd = (list(col) for col in zip(*chains))

    tr = lax.broadcasted_iota(jnp.int32, (c, c), 0)
    tc = lax.broadcasted_iota(jnp.int32, (c, c), 1)
    tri = (tr >= tc).astype(BF16)
    cl = [_cumsum_rows(tri, x) for x in lw]
    cl_end = [x[c - 1:c, :] for x in cl]
    r_t = _each(lambda x, d: stack(x * jnp.exp(d)), r, cl)
    a_t = _each(lambda x, d, w: stack(x * jnp.exp(d - w)), a, cl, lw)
    b_t = _each(lambda x, d: stack(x * jnp.exp(-d)), b, cl)
    k_t = _each(lambda x, d: stack(x * jnp.exp(-d)), k, cl)
    bk_e = _each(lambda x, y, d, e: jnp.concatenate([stack(x * jnp.exp(e - d)), stack(y * jnp.exp(e - d))], axis=0),
                 b, k, cl, cl_end)
    v_s = [stack(x) for x in v]

    rr = lax.broadcasted_iota(jnp.int32, (n, n), 0)
    qq = lax.broadcasted_iota(jnp.int32, (n, n), 1)
    same = rr // c == qq // c
    strict = jnp.logical_and(same, rr > qq)
    incl = jnp.logical_and(same, rr >= qq)
    ar = _each(lambda x, y: jnp.concatenate([x, y], axis=0), a_t, r_t)
    gb = _mm_each(ar, b_t, _NT)
    gk = _mm_each(ar, k_t, _NT)
    ab = [jnp.where(strict, g[:n], 0.0) for g in gb]
    rb = [jnp.where(incl, g[n:], 0.0) for g in gb]
    ak = [jnp.where(strict, g[:n], 0.0) for g in gk]
    rk = [jnp.where(incl, g[n:], 0.0) for g in gk]

    us = _mm_each(ar, s_bd, _NT)
    akv = _mm_each(ak, v_s, _NN)
    rkv = _mm_each(rk, v_s, _NN)
    t_inv = _tri_inverse(ab, c)
    u = _mm_each(t_inv, _each(lambda x, y: x[:n] + y, us, akv), _NN)
    rbu = _mm_each(rb, u, _NN)
    y_s = _each(lambda x, y, z: x[n:] + y + z, us, rbu, rkv)
    upd = _mm_each(_each(lambda x, y: jnp.concatenate([x, y], axis=0), u, v_s), bk_e, _TN)
    s_new = _each(lambda s, e, d: s * jnp.exp(e) + d, s_bd, cl_end, upd)
    return [(y[:c] + y[c:], s) for y, s in zip(y_s, s_new)]


RWKV_PAIRS = 16


def _rwkv_prompt_kernel(r_ref, lw_ref, k_ref, v_ref, a_ref, b_ref, y_ref, s_ref, s_acc):
    ci = pl.program_id(2)

    @pl.when(ci == 0)
    def _():
        s_acc[...] = jnp.zeros_like(s_acc)

    lanes = [slice(p * LANES, (p + 1) * LANES) for p in range(RWKV_PAIRS)]
    chains = [(r_ref[:, sl], lw_ref[:, sl], k_ref[:, sl], v_ref[:, sl], a_ref[:, sl], b_ref[:, sl], s_acc[p])
              for p, sl in enumerate(lanes)]
    res = _rwkv_chunk(chains, CHUNK)
    for p, sl in enumerate(lanes):
        y_ref[:, sl] = res[p][0]
        s_acc[p] = res[p][1]

    @pl.when(ci == pl.num_programs(2) - 1)
    def _():
        for p in range(RWKV_PAIRS):
            _store_pair_state(s_ref, p, s_acc[p])


def _pair_state(s_ref, p):
    zero = jnp.zeros((HEAD_B, HEAD_B), F32)
    top = jnp.concatenate([s_ref[2 * p], zero], axis=1)
    bot = jnp.concatenate([zero, s_ref[2 * p + 1]], axis=1)
    return jnp.concatenate([top, bot], axis=0)


def _store_pair_state(s_ref, p, s_bd):
    s_ref[2 * p] = s_bd[:HEAD_B, :HEAD_B]
    s_ref[2 * p + 1] = s_bd[HEAD_B:, HEAD_B:]


def _rwkv_prompt(r, lw, k, v, a, b, batch, seq):
    npair = B_WIDTH // LANES
    nchunk = seq // CHUNK
    wide = RWKV_PAIRS * LANES
    spec = pl.BlockSpec((CHUNK, wide), lambda bi, p, ci: (bi * nchunk + ci, p))
    return pl.pallas_call(
        _rwkv_prompt_kernel,
        out_shape=(jax.ShapeDtypeStruct((batch * seq, B_WIDTH), F32),
                   jax.ShapeDtypeStruct((batch, 2 * npair, HEAD_B, HEAD_B), F32)),
        grid=(batch, npair // RWKV_PAIRS, nchunk),
        in_specs=[spec] * 6,
        out_specs=(spec, pl.BlockSpec((None, 2 * RWKV_PAIRS, HEAD_B, HEAD_B), lambda bi, p, ci: (bi, p, 0, 0))),
        scratch_shapes=[pltpu.VMEM((RWKV_PAIRS, LANES, LANES), F32)],
        compiler_params=_cparams(("parallel", "parallel", "arbitrary")),
        name="rwkv_prompt",
    )(r, lw, k, v, a, b)


def _rwkv_sample_kernel(r_ref, lw_ref, k_ref, v_ref, a_ref, b_ref, s0_ref, y_ref, s_ref, *, t_new):
    c = SAMPLE_CHUNK
    row = lax.broadcasted_iota(jnp.int32, (c, LANES), 0)
    def sel(ref, sl, e):
        x = ref[:, sl]
        if e:
            x = pltpu.roll(x, c - e * t_new, 0)
        return jnp.where(row < t_new, x, 0.0)

    ids = [(p, e) for p in range(RWKV_PAIRS) for e in range(SAMPLE_PAIR)]
    chains = []
    for p, e in ids:
        sl = slice(p * LANES, (p + 1) * LANES)
        chains.append(tuple(sel(ref, sl, e) for ref in (r_ref, lw_ref, k_ref, v_ref, a_ref, b_ref))
                      + (_pair_state(s0_ref.at[e], p),))
    res = dict(zip(ids, _rwkv_chunk(chains, c)))
    for p in range(RWKV_PAIRS):
        y_all = res[(p, 0)][0]
        for e in range(1, SAMPLE_PAIR):
            y_all = jnp.where(row // t_new == e, pltpu.roll(res[(p, e)][0], e * t_new, 0), y_all)
        y_ref[:, p * LANES:(p + 1) * LANES] = y_all
        for e in range(SAMPLE_PAIR):
            _store_pair_state(s_ref.at[e], p, res[(p, e)][1])


def _rwkv_sample(r, lw, k, v, a, b, s0, row0, dec_batch, t_new):
    npair = B_WIDTH // LANES
    nrow = SAMPLE_PAIR * t_new
    rb0 = row0 // nrow
    wide = RWKV_PAIRS * LANES
    spec = pl.BlockSpec((nrow, wide), lambda i, p: (rb0 + i, p))
    s_spec = pl.BlockSpec((SAMPLE_PAIR, 2 * RWKV_PAIRS, HEAD_B, HEAD_B), lambda i, p: (i, p, 0, 0))
    return pl.pallas_call(
        functools.partial(_rwkv_sample_kernel, t_new=t_new),
        out_shape=(jax.ShapeDtypeStruct((dec_batch * t_new, B_WIDTH), F32),
                   jax.ShapeDtypeStruct(s0.shape, F32)),
        grid=(dec_batch // SAMPLE_PAIR, npair // RWKV_PAIRS),
        in_specs=[spec] * 6 + [s_spec],
        out_specs=(pl.BlockSpec((nrow, wide), lambda i, p: (i, p)), s_spec),
        compiler_params=_cparams(("parallel", "parallel")),
        name="rwkv_sample",
    )(r, lw, k, v, a, b, s0)


def _rwkv_post_kernel(y_ref, r_ref, k_ref, v_ref, g_ref, vec_ref, o_ref):
    ones_bd = _segsum_mat()
    inv = 1.0 / HEAD_B
    for s in range(B_WIDTH // LANES):
        sl = slice(s * LANES, (s + 1) * LANES)
        y = y_ref[:, sl]
        mean = _segsum(y, ones_bd) * inv
        dlt = y - mean
        var = _segsum(dlt * dlt, ones_bd) * inv
        yn = dlt * lax.rsqrt(var + GN_EPS) * vec_ref[0:1, sl] + vec_ref[1:2, sl]
        bonus = _segsum(r_ref[:, sl] * k_ref[:, sl] * vec_ref[2:3, sl], ones_bd) * v_ref[:, sl]
        o_ref[:, sl] = ((yn + bonus) * g_ref[:, sl]).astype(o_ref.dtype)


def _rwkv_post(y, r, k_mod, v, g, vecs, tm=256):
    n = y.shape[0]
    wide = pl.BlockSpec((tm, B_WIDTH), lambda i: (i, 0))
    return pl.pallas_call(
        _rwkv_post_kernel,
        out_shape=jax.ShapeDtypeStruct((n, B_WIDTH), BF16),
        grid=(n // tm,),
        in_specs=[wide] * 5 + [pl.BlockSpec((3, B_WIDTH), lambda i: (0, 0))],
        out_specs=wide,
        compiler_params=_cparams(("parallel",)),
        name="rwkv_post",
    )(y, r, k_mod, v, g, vecs)


def _read_kernel(ao_ref, bo_ref, wa_ref, wb_ref, ga_ref, gb_ref, o_ref):
    ra = jnp.dot(ao_ref[...], wa_ref[...].astype(BF16), preferred_element_type=F32)
    rb = jnp.dot(bo_ref[...], wb_ref[...].astype(BF16), preferred_element_type=F32)
    o_ref[...] = (ga_ref[...] * ra + gb_ref[...] * rb).astype(o_ref.dtype)


def _gated_read(a_out, b_out, w_read_a, w_read_b, gates, tm, tn=512):
    n = a_out.shape[0]
    ncol = D_MODEL // tn
    return pl.pallas_call(
        _read_kernel,
        out_shape=jax.ShapeDtypeStruct((n, D_MODEL), BF16),
        grid=(ncol, n // tm),
        in_specs=[pl.BlockSpec((tm, GROUP_W), lambda j, i: (i, 0)),
                  pl.BlockSpec((tm, B_WIDTH), lambda j, i: (i, 0)),
                  pl.BlockSpec((GROUP_W, tn), lambda j, i: (0, j)),
                  pl.BlockSpec((B_WIDTH, tn), lambda j, i: (0, j)),
                  pl.BlockSpec((tm, tn), lambda j, i: (i, j)),
                  pl.BlockSpec((tm, tn), lambda j, i: (i, j + ncol))],
        out_specs=pl.BlockSpec((tm, tn), lambda j, i: (i, j)),
        compiler_params=_cparams(("parallel", "parallel")),
        name="gated_read",
    )(a_out, b_out, w_read_a, w_read_b, gates, gates)


def _residual_epilogue(acc, x_ref):
    return x_ref[...] + acc


def _pack_halves(h):
    half = h.shape[1] // 2
    top = pltpu.bitcast(h[:, :half].astype(BF16).astype(F32), jnp.uint32)
    bot = pltpu.bitcast(h[:, half:].astype(BF16).astype(F32), jnp.uint32)
    return top | (bot >> 16)


def _unpack_halves(p):
    top = pltpu.bitcast(p & jnp.uint32(0xFFFF0000), F32).astype(BF16)
    bot = pltpu.bitcast(p << 16, F32).astype(BF16)
    return top, bot


def _ffn_norm_router_kernel(u_ref, g_ref, wr_ref, hb_ref, lg_ref):
    x = u_ref[...]
    ms = jnp.mean(x * x, axis=-1, keepdims=True)
    h = x * lax.rsqrt(ms + NORM_EPS) * g_ref[...]
    hb_ref[...] = _pack_halves(h)
    lg_ref[...] = _dot3(h, wr_ref[...], _NN)


def _ffn_norm_router(u, gain, w_router, tm=256):
    n, d = u.shape
    return pl.pallas_call(
        _ffn_norm_router_kernel,
        out_shape=(jax.ShapeDtypeStruct((n, d // 2), jnp.uint32), jax.ShapeDtypeStruct((n, ROUTER_PAD), F32)),
        grid=(n // tm,),
        in_specs=[pl.BlockSpec((tm, d), lambda i: (i, 0)), pl.BlockSpec((1, d), lambda i: (0, 0)),
                  pl.BlockSpec((d, ROUTER_PAD), lambda i: (0, 0))],
        out_specs=(pl.BlockSpec((tm, d // 2), lambda i: (i, 0)), pl.BlockSpec((tm, ROUTER_PAD), lambda i: (i, 0))),
        compiler_params=_cparams(("parallel",)),
        name="ffn_norm_router",
    )(u, gain.reshape(1, d), w_router)


def _route(logits, group_b, expert_b):
    n = logits.shape[0]
    group_logits = logits[:, :N_EXPERT_GROUPS] + group_b.astype(F32)
    group = jnp.argmax(group_logits, axis=-1).astype(jnp.int32)
    p_group = jnp.take_along_axis(jax.nn.softmax(group_logits, axis=-1), group[:, None], axis=-1)
    expert_logits = (logits[:, N_EXPERT_GROUPS:N_EXPERT_GROUPS + N_EXPERTS] + expert_b.astype(F32)).reshape(
        n, N_EXPERT_GROUPS, EXPERTS_PER_GROUP)
    in_group = jnp.take_along_axis(expert_logits, group[:, None, None], axis=1)[:, 0]
    top_val, top_idx = lax.top_k(in_group, TOP_K)
    gate = p_group * jax.nn.softmax(top_val, axis=-1)
    expert_id = group[:, None] * EXPERTS_PER_GROUP + top_idx.astype(jnp.int32)
    return expert_id, gate


FFN_CHUNKS = 2
FFN_VMEM = 56 * 1024 * 1024


def _moe_plan(expert_id):
    n_assign = expert_id.size
    n_blocks = n_assign // ROW_BLOCK + N_EXPERTS
    n_rows = n_blocks * ROW_BLOCK
    max_items = FFN_CHUNKS * n_blocks
    i32 = jnp.int32
    e_flat = expert_id.reshape(n_assign)
    onehot = (e_flat[:, None] == jnp.arange(N_EXPERTS, dtype=i32)[None, :]).astype(i32)
    running = jnp.cumsum(onehot, axis=0)
    rank = jnp.sum(onehot * (running - 1), axis=1)
    counts = running[-1]
    nblk = (counts + ROW_BLOCK - 1) // ROW_BLOCK
    blk_start = jnp.cumsum(nblk) - nblk
    pos = (blk_start[e_flat] * ROW_BLOCK + rank).astype(i32)
    row_src = jnp.zeros((n_rows,), i32).at[pos].set(jnp.arange(n_assign, dtype=i32) // TOP_K)
    item_cnt = FFN_CHUNKS * nblk
    item_end = jnp.cumsum(item_cnt)
    item_start = item_end - item_cnt
    total = item_end[-1]
    idx = jnp.arange(max_items, dtype=i32)
    ic = jnp.minimum(idx, total - 1)
    it_e = jnp.minimum(jnp.sum((item_end[None, :] <= ic[:, None]).astype(i32), axis=1), N_EXPERTS - 1)
    within = ic - item_start[it_e]
    nb = jnp.maximum(nblk[it_e], 1)
    it_wc = within // nb
    valid = idx < total
    first = jnp.logical_and(valid, within % nb == 0)
    tail = idx - total
    it_blk = jnp.where(valid, blk_start[it_e] + within % nb, total // FFN_CHUNKS + tail // FFN_CHUNKS)
    it_c = jnp.where(valid, it_wc, tail % FFN_CHUNKS)
    return dict(pos=pos, row_src=row_src, it_e=it_e, it_wc=it_wc.astype(i32), it_c=it_c.astype(i32),
                it_blk=it_blk.astype(i32), it_valid=valid.astype(i32), it_first=first.astype(i32),
                n_rows=n_rows, max_items=max_items)


GATHER_ROWS = 128


def _dispatch_kernel(src_ref, x_hbm, o_ref, buf, sem):
    i = pl.program_id(0)
    slot = i % 2

    def row_copy(slot_, j, src_row):
        return pltpu.make_async_copy(x_hbm.at[pl.ds(src_row, 1)], buf.at[slot_, pl.ds(j, 1)], sem.at[slot_])

    def issue(block, slot_):
        base = block * GATHER_ROWS

        def body(j, carry):
            row_copy(slot_, j, src_ref[base + j]).start()
            return carry

        lax.fori_loop(0, GATHER_ROWS, body, 0, unroll=8)

    @pl.when(i == 0)
    def _():
        issue(0, 0)

    @pl.when(i + 1 < pl.num_programs(0))
    def _():
        issue(i + 1, 1 - slot)

    for j in range(GATHER_ROWS):
        row_copy(slot, j, 0).wait()
    o_ref[...] = buf[slot]


def _dispatch_rows(x, row_src, n_rows):
    d = x.shape[1]
    return pl.pallas_call(
        _dispatch_kernel,
        out_shape=jax.ShapeDtypeStruct((n_rows, d), x.dtype),
        grid_spec=pltpu.PrefetchScalarGridSpec(
            num_scalar_prefetch=1,
            grid=(n_rows // GATHER_ROWS,),
            in_specs=[pl.BlockSpec(memory_space=pl.ANY)],
            out_specs=pl.BlockSpec((GATHER_ROWS, d), lambda i, src: (i, 0)),
            scratch_shapes=[pltpu.VMEM((2, GATHER_ROWS, d), x.dtype), pltpu.SemaphoreType.DMA((2,))],
        ),
        compiler_params=_cparams(("arbitrary",)),
        name="moe_dispatch",
    )(row_src, x)


def _ffn_up_kernel(e_ref, wc_ref, c_ref, blk_ref, valid_ref, first_ref, x_ref, wg_ref, wu_ref, h_ref, wg_bf, wu_bf):
    i = pl.program_id(0)

    @pl.when(first_ref[i] == 1)
    def _():
        wg_bf[...] = wg_ref[...].astype(BF16)
        wu_bf[...] = wu_ref[...].astype(BF16)

    @pl.when(valid_ref[i] == 0)
    def _():
        h_ref[...] = jnp.zeros_like(h_ref)

    @pl.when(valid_ref[i] == 1)
    def _():
        top, bot = _unpack_halves(x_ref[...])
        half = top.shape[1]
        gate = (jnp.dot(top, wg_bf[:half, :], preferred_element_type=F32)
                + jnp.dot(bot, wg_bf[half:, :], preferred_element_type=F32))
        up = (jnp.dot(top, wu_bf[:half, :], preferred_element_type=F32)
              + jnp.dot(bot, wu_bf[half:, :], preferred_element_type=F32))
        h_ref[...] = (jax.nn.silu(gate) * up).astype(h_ref.dtype)


def _ffn_down_kernel(e_ref, wc_ref, c_ref, blk_ref, valid_ref, first_ref, h_ref, wd_ref, o_ref, wd_bf):
    i = pl.program_id(0)

    @pl.when(first_ref[i] == 1)
    def _():
        wd_bf[...] = wd_ref[...].astype(BF16)

    @pl.when(valid_ref[i] == 0)
    def _():
        o_ref[...] = jnp.zeros_like(o_ref)

    @pl.when(valid_ref[i] == 1)
    def _():
        o_ref[...] = jnp.dot(h_ref[...], wd_bf[...], preferred_element_type=F32)


def _expert_ffn(x_sorted, plan, w_gate, w_up, w_down):
    n_rows, dh = x_sorted.shape
    d = 2 * dh
    fc = D_EXPERT // FFN_CHUNKS
    dc = d // FFN_CHUNKS
    prefetch = (plan["it_e"], plan["it_wc"], plan["it_c"], plan["it_blk"], plan["it_valid"], plan["it_first"])
    grid = (plan["max_items"],)
    hid = pl.pallas_call(
        _ffn_up_kernel,
        out_shape=jax.ShapeDtypeStruct((n_rows, D_EXPERT), BF16),
        grid_spec=pltpu.PrefetchScalarGridSpec(
            num_scalar_prefetch=len(prefetch),
            grid=grid,
            in_specs=[pl.BlockSpec((ROW_BLOCK, dh), lambda i, e, wc, c, b, v, f: (b[i], 0)),
                      pl.BlockSpec((None, d, fc), lambda i, e, wc, c, b, v, f: (e[i], 0, wc[i])),
                      pl.BlockSpec((None, d, fc), lambda i, e, wc, c, b, v, f: (e[i], 0, wc[i]))],
            out_specs=pl.BlockSpec((ROW_BLOCK, fc), lambda i, e, wc, c, b, v, f: (b[i], c[i])),
            scratch_shapes=[pltpu.VMEM((d, fc), BF16), pltpu.VMEM((d, fc), BF16)],
        ),
        compiler_params=_cparams(("arbitrary",), vmem=FFN_VMEM),
        name="expert_ffn_up",
    )(*prefetch, x_sorted, w_gate, w_up)
    out = pl.pallas_call(
        _ffn_down_kernel,
        out_shape=jax.ShapeDtypeStruct((n_rows, d), F32),
        grid_spec=pltpu.PrefetchScalarGridSpec(
            num_scalar_prefetch=len(prefetch),
            grid=grid,
            in_specs=[pl.BlockSpec((ROW_BLOCK, D_EXPERT), lambda i, e, wc, c, b, v, f: (b[i], 0)),
                      pl.BlockSpec((None, D_EXPERT, dc), lambda i, e, wc, c, b, v, f: (e[i], 0, wc[i]))],
            out_specs=pl.BlockSpec((ROW_BLOCK, dc), lambda i, e, wc, c, b, v, f: (b[i], c[i])),
            scratch_shapes=[pltpu.VMEM((D_EXPERT, dc), BF16)],
        ),
        compiler_params=_cparams(("arbitrary",)),
        name="expert_ffn_down",
    )(*prefetch, hid, w_down)
    return out


COMBINE_ROWS = 64


def _combine_kernel(pos_ref, u_ref, g_ref, y_hbm, o_ref, buf, sem):
    i = pl.program_id(0)
    slot = i % 2

    def row_copy(slot_, s, j, src_row):
        return pltpu.make_async_copy(y_hbm.at[pl.ds(src_row, 1)], buf.at[slot_, s, pl.ds(j, 1)], sem.at[slot_])

    def issue(tile, slot_):
        base = tile * COMBINE_ROWS

        def body(j, carry):
            for s in range(TOP_K):
                row_copy(slot_, s, j, pos_ref[(base + j) * TOP_K + s]).start()
            return carry

        lax.fori_loop(0, COMBINE_ROWS, body, 0, unroll=8)

    @pl.when(i == 0)
    def _():
        issue(0, 0)

    @pl.when(i + 1 < pl.num_programs(0))
    def _():
        issue(i + 1, 1 - slot)

    for j in range(COMBINE_ROWS):
        for s in range(TOP_K):
            row_copy(slot, s, j, 0).wait()
    g = g_ref[...]
    moe = buf[slot, 0] * g[:, 0:1] + buf[slot, 1] * g[:, 1:2]
    o_ref[...] = u_ref[...] + moe


def _combine(u, gate, y_sorted, pos):
    n, d = u.shape
    return pl.pallas_call(
        _combine_kernel,
        out_shape=jax.ShapeDtypeStruct((n, d), F32),
        grid_spec=pltpu.PrefetchScalarGridSpec(
            num_scalar_prefetch=1,
            grid=(n // COMBINE_ROWS,),
            in_specs=[pl.BlockSpec((COMBINE_ROWS, d), lambda i, p: (i, 0)),
                      pl.BlockSpec((COMBINE_ROWS, TOP_K), lambda i, p: (i, 0)),
                      pl.BlockSpec(memory_space=pl.ANY)],
            out_specs=pl.BlockSpec((COMBINE_ROWS, d), lambda i, p: (i, 0)),
            scratch_shapes=[pltpu.VMEM((2, TOP_K, COMBINE_ROWS, d), F32), pltpu.SemaphoreType.DMA((2,))],
        ),
        compiler_params=_cparams(("arbitrary",)),
        name="moe_combine",
    )(pos, u, gate, y_sorted)


def _pick_tm(n):
    for tm in (1088, 1024, 544, 512, 272, 256, 128):
        if n % tm == 0:
            return tm
    raise ValueError(f"row count {n} is not a multiple of 128")


def kernel(x_prompt, x_sample, cache_kv_w128, cache_kv_w512, cache_kv_w2048, state_shift, state_wkv, norm_mix, w_in, q_norm, k_norm, mu_shift, rwkv_w0, rwkv_w2, rwkv_a0, rwkv_a2, rwkv_g2, rwkv_k_k, rwkv_k_a, rwkv_r_k, rwkv_ln_w, rwkv_ln_b, w_read_a, w_read_b, w_o, norm_ffn, router_group_w, router_group_b, router_expert_w, router_expert_b, expert_w_gate, expert_w_up, expert_w_down):
    batch, seq, d = x_prompt.shape
    dec_batch, t_new, _ = x_sample.shape
    n_p, n_s = batch * seq, dec_batch * t_new
    n = n_p + n_s
    past = cache_kv_w2048.shape[1]
    assert d == D_MODEL and seq % (DILATIONS[2] * ROW_BLOCK) == 0 and seq % CHUNK == 0
    assert cache_kv_w128.shape[1] == 128 and cache_kv_w512.shape[1] == 512 and past == 2048
    tm = _pick_tm(n)

    x = jnp.concatenate([x_prompt.reshape(n_p, d), x_sample.reshape(n_s, d)], axis=0)
    h = _rmsnorm(x, norm_mix)
    shift_p = h[seq - 1:n_p:seq]
    shift_s = h[n_p + t_new - 1::t_new]
    first_rows = jnp.broadcast_to(state_shift[:, None].astype(F32), (dec_batch, t_new, d)).reshape(n_s, d)
    hb, mixes = _token_shift_mix(h, first_rows, mu_shift, n_p, seq, t_new)

    positions = jnp.concatenate([jnp.tile(jnp.arange(seq, dtype=jnp.int32), batch),
                                 jnp.tile(past + jnp.arange(t_new, dtype=jnp.int32), dec_batch)])
    rope_c, rope_a, rope_b = _rope_tables(positions)
    gains = jnp.stack([q_norm, k_norm]).reshape(2, 1, HEAD_A).astype(F32)
    tn = GROUP_W
    row_spec = pl.BlockSpec((tm, HEAD_A), lambda j, i: (i, 0))
    w_in_t = w_in.T
    proj = functools.partial(_matmul, w=w_in_t, w_transposed=True, tm=tm)
    qk = proj(hb, col_off=0, width=2 * A_WIDTH, tn=tn, epilogue=_qk_epilogue,
              extra=(gains, rope_c, rope_a, rope_b),
              extra_specs=(pl.BlockSpec((None, 1, HEAD_A), lambda j, i: (j // N_GROUPS, 0, 0)),
                           row_spec, row_spec, row_spec), name="proj_qk")
    v_a = proj(hb, col_off=2 * A_WIDTH, width=A_WIDTH, tn=tn, name="proj_v")
    gates = proj(hb, col_off=3 * A_WIDTH, width=2 * D_MODEL, tn=tn, epilogue=_sigmoid_epilogue, name="proj_gates")
    r_b = proj(mixes, a_sel=0, col_off=X_COLS, width=B_WIDTH, tn=tn, name="proj_r")
    k_b = proj(mixes, a_sel=1, col_off=X_COLS + B_WIDTH, width=B_WIDTH, tn=tn, name="proj_k")
    v_b = proj(mixes, a_sel=2, col_off=X_COLS + 2 * B_WIDTH, width=B_WIDTH, tn=tn, name="proj_vb")
    lora0 = X_COLS + 3 * B_WIDTH
    w_low = proj(mixes, a_sel=3, col_off=lora0, width=DECAY_LORA, tn=DECAY_LORA, name="proj_wlow")
    a_low = proj(mixes, a_sel=4, col_off=lora0 + DECAY_LORA, width=AAA_LORA, tn=AAA_LORA, name="proj_alow")
    g_low = proj(mixes, a_sel=5, col_off=lora0 + DECAY_LORA + AAA_LORA, width=GATE_LORA_PAD, tn=DECAY_LORA,
                 epilogue=functools.partial(_valid_cols_epilogue, valid=GATE_LORA), name="proj_glow")

    outs, lses = [], []
    for g in range(N_GROUPS):
        o_g, l_g = _prompt_group_attention(qk, v_a, g, batch, seq)
        outs.append(o_g)
        lses.append(l_g)
    a_out_p = _merge_groups(outs, lses)
    caches = (cache_kv_w128, cache_kv_w512, cache_kv_w2048)
    a_out_s = _sample_attention(qk, v_a, caches, n_p, dec_batch, t_new)
    a_out = jnp.concatenate([a_out_p, a_out_s], axis=0)

    def kv_rows(g, rows):
        kk = qk[rows, A_WIDTH + g * GROUP_W:A_WIDTH + (g + 1) * GROUP_W]
        vv = v_a[rows, g * GROUP_W:(g + 1) * GROUP_W]
        return jnp.stack([kk.reshape(-1, HEADS_PER_GROUP, HEAD_A), vv.reshape(-1, HEADS_PER_GROUP, HEAD_A)], axis=1)

    kv_prompt = []
    for g in range(N_GROUPS):
        keep = min(caches[g].shape[1], seq)
        full = kv_rows(g, slice(0, n_p)).reshape(batch, seq, 2, HEADS_PER_GROUP, HEAD_A)
        kv_prompt.append(full[:, seq - keep:])
    news = [kv_rows(g, slice(n_p, n)).reshape(dec_batch, t_new, 2, HEADS_PER_GROUP, HEAD_A) for g in range(N_GROUPS)]
    kv_sample = _roll_caches(caches, news, t_new)

    g2_pad = jnp.pad(rwkv_g2, ((0, GATE_LORA_PAD - GATE_LORA), (0, 0)))
    vec_prep = jnp.stack([rwkv_w0, rwkv_a0, rwkv_k_k, rwkv_k_a]).astype(F32)
    lw, k_mod, neg_kk, kk_a, g_out = _rwkv_prep(k_b, w_low, a_low, g_low, rwkv_w2, rwkv_a2, g2_pad, vec_prep)
    y_p, s_p = _rwkv_prompt(r_b, lw, k_mod, v_b, neg_kk, kk_a, batch, seq)
    y_s, s_s = _rwkv_sample(r_b, lw, k_mod, v_b, neg_kk, kk_a, state_wkv.astype(F32), n_p, dec_batch, t_new)
    y_b = jnp.concatenate([y_p, y_s], axis=0)
    vec_post = jnp.stack([rwkv_ln_w, rwkv_ln_b, rwkv_r_k.reshape(B_WIDTH)]).astype(F32)
    b_out = _rwkv_post(y_b, r_b, k_mod, v_b, g_out, vec_post)

    merged = _gated_read(a_out, b_out, w_read_a, w_read_b, gates, tm)
    u = _matmul(merged, w_o, tm=tm, tn=tn, epilogue=_residual_epilogue, extra=(x,),
                extra_specs=(pl.BlockSpec((tm, tn), lambda j, i: (i, j)),), name="proj_out")
    w_router = jnp.pad(jnp.concatenate([router_group_w, router_expert_w], axis=1),
                       ((0, 0), (0, ROUTER_PAD - N_EXPERT_GROUPS - N_EXPERTS)))
    hn, logits = _ffn_norm_router(u, norm_ffn, w_router)
    expert_id, gate = _route(logits, router_group_b, router_expert_b)
    plan = _moe_plan(expert_id)
    x_sorted = _dispatch_rows(hn, plan["row_src"], plan["n_rows"])
    y_sorted = _expert_ffn(x_sorted, plan, expert_w_gate, expert_w_up, expert_w_down)
    y = _combine(u, gate, y_sorted, plan["pos"])

    return (y[:n_p].reshape(batch, seq, d), y[n_p:].reshape(dec_batch, t_new, d),
            kv_prompt[0], kv_prompt[1], kv_prompt[2], shift_p, s_p.astype(state_wkv.dtype),
            kv_sample[0], kv_sample[1], kv_sample[2], shift_s, s_s.astype(state_wkv.dtype))
```

```python
import functools

import jax
import jax.numpy as jnp
from jax import lax
from jax.experimental import pallas as pl
from jax.experimental.pallas import tpu as pltpu

F32 = jnp.float32
BF16 = jnp.bfloat16

LANES = 128
VMEM_LIMIT = 48 * 1024 * 1024

D_MODEL = 4096
HEAD_A = 128
N_GROUPS = 3
HEADS_PER_GROUP = 4
GROUP_W = HEADS_PER_GROUP * HEAD_A
A_WIDTH = N_GROUPS * GROUP_W
DILATIONS = (1, 4, 16)
WINDOW_KEYS = 128
ROPE_THETA = 500000.0
ROPE_DIM = HEAD_A // 4
ROPE_HALF = ROPE_DIM // 2
HEAD_B = 64
B_WIDTH = D_MODEL // 2
DECAY_LORA = 128
AAA_LORA = 128
GATE_LORA = 480
GATE_LORA_PAD = 512
GN_EPS = 64e-5
NORM_EPS = 1e-6
X_COLS = 3 * A_WIDTH + 2 * D_MODEL
N_EXPERT_GROUPS = 8
EXPERTS_PER_GROUP = 8
N_EXPERTS = 64
TOP_K = 2
D_EXPERT = 1024
ROW_BLOCK = 128
ROUTER_PAD = 128
CHUNK = 64
SAMPLE_CHUNK = 8


def _cparams(sem, vmem=VMEM_LIMIT):
    return pltpu.CompilerParams(dimension_semantics=sem, vmem_limit_bytes=vmem)


def _rmsnorm_kernel(x_ref, g_ref, o_ref):
    x = x_ref[...]
    ms = jnp.mean(x * x, axis=-1, keepdims=True)
    o_ref[...] = x * lax.rsqrt(ms + NORM_EPS) * g_ref[...]


def _rmsnorm(x, gain, tm=256):
    n, d = x.shape
    return pl.pallas_call(
        _rmsnorm_kernel,
        out_shape=jax.ShapeDtypeStruct((n, d), F32),
        grid=(n // tm,),
        in_specs=[pl.BlockSpec((tm, d), lambda i: (i, 0)), pl.BlockSpec((1, d), lambda i: (0, 0))],
        out_specs=pl.BlockSpec((tm, d), lambda i: (i, 0)),
        compiler_params=_cparams(("parallel",)),
        name="rmsnorm",
    )(x, gain.reshape(1, d))


SUBLANES = 8


def _mix_kernel(h_ref, before_ref, first_ref, mu_ref, hb_ref, m_ref, *, n_prompt, seq, t_new):
    i = pl.program_id(0)
    tm = h_ref.shape[0]
    h = h_ref[...]
    row = lax.broadcasted_iota(jnp.int32, (tm, 1), 0)
    prev = jnp.where(row == 0, before_ref[SUBLANES - 1:SUBLANES, :], pltpu.roll(h, 1, 0))
    is_sample = i * tm >= n_prompt
    sample_start = jnp.where(row % t_new == 0, 1, 0)
    prompt_start = jnp.where((i * tm + row) % seq == 0, 1, 0)
    starts = jnp.where(is_sample, sample_start, prompt_start) == 1
    first = jnp.where(is_sample, first_ref[...], 0.0)
    xx = jnp.where(starts, first, prev) - h
    hb_ref[...] = h.astype(BF16)
    for j in range(6):
        m_ref[j] = (h + xx * mu_ref[j:j + 1, :]).astype(BF16)


def _token_shift_mix(h, first_rows, mu, n_prompt, seq, t_new, tm=128):
    n, d = h.shape
    assert n_prompt % tm == 0 and seq % tm == 0 and tm % t_new == 0 and tm % SUBLANES == 0
    per = tm // SUBLANES
    kern = functools.partial(_mix_kernel, n_prompt=n_prompt, seq=seq, t_new=t_new)
    return pl.pallas_call(
        kern,
        out_shape=(jax.ShapeDtypeStruct((n, d), BF16), jax.ShapeDtypeStruct((6, n, d), BF16)),
        grid=(n // tm,),
        in_specs=[pl.BlockSpec((tm, d), lambda i: (i, 0)),
                  pl.BlockSpec((SUBLANES, d), lambda i: (jnp.maximum(i * per - 1, 0), 0)),
                  pl.BlockSpec((tm, d), lambda i: (jnp.maximum(i - n_prompt // tm, 0), 0)),
                  pl.BlockSpec((6, d), lambda i: (0, 0))],
        out_specs=(pl.BlockSpec((tm, d), lambda i: (i, 0)), pl.BlockSpec((6, tm, d), lambda i: (0, i, 0))),
        compiler_params=_cparams(("parallel",)),
        name="token_shift_mix",
    )(h, h, first_rows, mu)


def _mm_kernel(*refs, n_extra, epilogue, w_transposed):
    a_ref, w_ref = refs[0], refs[1]
    extra = refs[2:2 + n_extra]
    o_ref = refs[2 + n_extra]
    dims = _NT if w_transposed else _NN
    acc = lax.dot_general(a_ref[...], w_ref[...].astype(BF16), (dims, ((), ())), preferred_element_type=F32)
    o_ref[...] = epilogue(acc, *extra).astype(o_ref.dtype)


def _matmul(a, w, *, col_off=0, width=None, tm, tn, epilogue=None, extra=(), extra_specs=(),
            out_dtype=F32, a_sel=None, w_transposed=False, name="matmul"):
    if a_sel is None:
        n, k = a.shape
        a_spec = pl.BlockSpec((tm, k), lambda j, i: (i, 0))
    else:
        _, n, k = a.shape
        a_spec = pl.BlockSpec((None, tm, k), lambda j, i: (a_sel, i, 0))
    n_cols = w.shape[0] if w_transposed else w.shape[1]
    width = n_cols - col_off if width is None else width
    assert col_off % tn == 0 and width % tn == 0 and n % tm == 0
    off = col_off // tn
    if epilogue is None:
        epilogue = lambda acc: acc
    kern = functools.partial(_mm_kernel, n_extra=len(extra), epilogue=epilogue, w_transposed=w_transposed)
    w_spec = (pl.BlockSpec((tn, k), lambda j, i: (j + off, 0)) if w_transposed
              else pl.BlockSpec((k, tn), lambda j, i: (0, j + off)))
    return pl.pallas_call(
        kern,
        out_shape=jax.ShapeDtypeStruct((n, width), out_dtype),
        grid=(width // tn, n // tm),
        in_specs=[a_spec, w_spec] + list(extra_specs),
        out_specs=pl.BlockSpec((tm, tn), lambda j, i: (i, j)),
        compiler_params=_cparams(("parallel", "parallel")),
        name=name,
    )(a, w, *extra)


def _qk_epilogue(acc, gain_ref, c_ref, a_ref, b_ref):
    c, a, b = c_ref[...], a_ref[...], b_ref[...]
    g = gain_ref[...]
    outs = []
    for hh in range(HEADS_PER_GROUP):
        x = acc[:, hh * HEAD_A:(hh + 1) * HEAD_A]
        ms = jnp.mean(x * x, axis=-1, keepdims=True)
        y = x * lax.rsqrt(ms + NORM_EPS) * g
        y = y * c + pltpu.roll(y, HEAD_A - ROPE_HALF, 1) * a + pltpu.roll(y, ROPE_HALF, 1) * b
        outs.append(y)
    return jnp.concatenate(outs, axis=1)


def _sigmoid_epilogue(acc):
    return jax.nn.sigmoid(acc)


def _valid_cols_epilogue(acc, *, valid):
    col = pl.program_id(0) * acc.shape[1] + lax.broadcasted_iota(jnp.int32, acc.shape, 1)
    return jnp.where(col < valid, acc, 0.0)


def _rope_tables(positions):
    n = positions.shape[0]
    inv_freq = ROPE_THETA ** (-jnp.arange(ROPE_HALF, dtype=F32) / ROPE_HALF)
    ang = positions.astype(F32)[:, None] * inv_freq[None, :]
    cos, sin = jnp.cos(ang), jnp.sin(ang)
    zeros = lambda w: jnp.zeros((n, w), F32)
    c = jnp.concatenate([cos, cos, jnp.ones((n, HEAD_A - ROPE_DIM), F32)], axis=1)
    a = jnp.concatenate([-sin, zeros(HEAD_A - ROPE_HALF)], axis=1)
    b = jnp.concatenate([zeros(ROPE_HALF), sin, zeros(HEAD_A - ROPE_DIM)], axis=1)
    return c, a, b


def _softmax_parts(s):
    m = jnp.max(s, axis=-1, keepdims=True)
    e = jnp.exp(s - m)
    denom = jnp.sum(e, axis=-1, keepdims=True)
    return e / denom, m + jnp.log(denom)


def _merge_three(outs, lses):
    m = jnp.maximum(jnp.maximum(lses[0], lses[1]), lses[2])
    es = [jnp.exp(l - m) for l in lses]
    tot = es[0] + es[1] + es[2]
    return (es[0] / tot) * outs[0] + (es[1] / tot) * outs[1] + (es[2] / tot) * outs[2]


def _prompt_attn_kernel(*refs, seq):
    ins, out_ref, o_s, l_s = refs[:3 * N_GROUPS], refs[3 * N_GROUPS], refs[3 * N_GROUPS + 1], refs[3 * N_GROUPS + 2]
    rows = lax.broadcasted_iota(jnp.int32, (ROW_BLOCK, ROW_BLOCK), 0)
    cols = lax.broadcasted_iota(jnp.int32, (ROW_BLOCK, ROW_BLOCK), 1)
    cur_ok = cols <= rows
    prev_ok = cols >= rows
    scale = HEAD_A ** -0.5
    nt = (((1,), (1,)), ((), ()))
    for g, d in enumerate(DILATIONS):
        q_ref, k_ref, v_ref = ins[3 * g:3 * g + 3]
        for c in range(d):
            for qb in range(seq // d // ROW_BLOCK):
                def class_rows(blk):
                    return pl.ds(blk * ROW_BLOCK, ROW_BLOCK) if d == 1 else pl.ds(c + blk * ROW_BLOCK * d, ROW_BLOCK, stride=d)

                sl = class_rows(qb)
                q = q_ref[sl, :].astype(BF16)
                v = v_ref[sl, :].astype(BF16)
                s = lax.dot_general(q, k_ref[sl, :].astype(BF16), nt, preferred_element_type=F32) * scale
                s = jnp.where(cur_ok, s, -jnp.inf)
                if qb > 0:
                    before = class_rows(qb - 1)
                    s_prev = lax.dot_general(q, k_ref[before, :].astype(BF16), nt, preferred_element_type=F32) * scale
                    s = jnp.concatenate([jnp.where(prev_ok, s_prev, -jnp.inf), s], axis=1)
                    v = jnp.concatenate([v_ref[before, :].astype(BF16), v], axis=0)
                p, lse = _softmax_parts(s)
                o_s[g, sl, :] = jnp.dot(p.astype(BF16), v, preferred_element_type=F32)
                l_s[g, sl, :] = jnp.broadcast_to(lse, (ROW_BLOCK, HEAD_A))
    out_ref[...] = _merge_three([o_s[g] for g in range(N_GROUPS)],
                                [l_s[g] for g in range(N_GROUPS)]).astype(out_ref.dtype)


def _prompt_attention(qk, v, batch, seq):
    heads = N_GROUPS * HEADS_PER_GROUP
    specs, args = [], []
    for g in range(N_GROUPS):
        specs += [pl.BlockSpec((seq, HEAD_A), lambda b, hh, g=g: (b, g * HEADS_PER_GROUP + hh)),
                  pl.BlockSpec((seq, HEAD_A), lambda b, hh, g=g: (b, heads + g * HEADS_PER_GROUP + hh)),
                  pl.BlockSpec((seq, HEAD_A), lambda b, hh, g=g: (b, g * HEADS_PER_GROUP + hh))]
        args += [qk, qk, v]
    return pl.pallas_call(
        functools.partial(_prompt_attn_kernel, seq=seq),
        out_shape=jax.ShapeDtypeStruct((batch * seq, GROUP_W), BF16),
        grid=(batch, HEADS_PER_GROUP),
        in_specs=specs,
        out_specs=pl.BlockSpec((seq, HEAD_A), lambda b, hh: (b, hh)),
        scratch_shapes=[pltpu.VMEM((N_GROUPS, seq, HEAD_A), F32), pltpu.VMEM((N_GROUPS, seq, HEAD_A), F32)],
        compiler_params=_cparams(("parallel", "parallel")),
        name="prompt_attn",
    )(*args)


SAMPLE_PAIR = 2
SAMPLE_ATTN_VMEM = 56 * 1024 * 1024


def _sample_attn_kernel(q_ref, k_ref, v_ref, c0_ref, c1_ref, c2_ref, out_ref, *, t_new):
    heads = HEADS_PER_GROUP
    scale = HEAD_A ** -0.5
    nq = heads * t_new
    qi = lax.broadcasted_iota(jnp.int32, (nq, 1), 0)
    q_h, q_t = qi // t_new, qi % t_new
    per_class = WINDOW_KEYS * heads
    out_rows = []
    for e in range(SAMPLE_PAIR):
        rows = slice(e * t_new, (e + 1) * t_new)
        outs, lses = [], []
        for g in range(N_GROUPS):
            def heads_to_rows(ref):
                return jnp.concatenate([ref[rows, (g * heads + h) * HEAD_A:(g * heads + h + 1) * HEAD_A]
                                        for h in range(heads)], axis=0)

            q = heads_to_rows(q_ref).astype(BF16)
            if g == 0:
                k_c = c0_ref[e, :, 0].reshape(per_class, HEAD_A)
                v_c = c0_ref[e, :, 1].reshape(per_class, HEAD_A)
                ncache = per_class
            else:
                cref = c1_ref if g == 1 else c2_ref
                k_c = jnp.concatenate([cref[e, :, c, 0].reshape(per_class, HEAD_A) for c in range(t_new)], axis=0)
                v_c = jnp.concatenate([cref[e, :, c, 1].reshape(per_class, HEAD_A) for c in range(t_new)], axis=0)
                ncache = t_new * per_class
            k_all = jnp.concatenate([k_c, heads_to_rows(k_ref)], axis=0).astype(BF16)
            v_all = jnp.concatenate([v_c, heads_to_rows(v_ref)], axis=0).astype(BF16)
            s = lax.dot_general(q, k_all, (((1,), (1,)), ((), ())), preferred_element_type=F32) * scale
            kj = lax.broadcasted_iota(jnp.int32, (1, ncache + nq), 1)
            is_new = kj >= ncache
            nj = kj - ncache
            k_h = jnp.where(is_new, nj // t_new, kj % heads)
            if g == 0:
                ok = jnp.where(is_new, q_t - nj % t_new, kj // heads - q_t) >= 0
            else:
                ok = jnp.where(is_new, nj % t_new, kj // per_class) == q_t
            s = jnp.where(jnp.logical_and(ok, k_h == q_h), s, -jnp.inf)
            p, lse = _softmax_parts(s)
            outs.append(jnp.dot(p.astype(BF16), v_all, preferred_element_type=F32))
            lses.append(lse)
        merged = _merge_three(outs, lses)
        out_rows.append(jnp.concatenate([merged[h * t_new:(h + 1) * t_new] for h in range(heads)], axis=1))
    out_ref[...] = jnp.concatenate(out_rows, axis=0).astype(out_ref.dtype)


def _sample_attention(qk, v, caches, row0, dec_batch, t_new):
    n = qk.shape[0]
    nrow = SAMPLE_PAIR * t_new
    assert nrow == 8 and row0 % nrow == 0 and dec_batch % SAMPLE_PAIR == 0
    rb0 = row0 // nrow
    tail = caches[0].shape[2:]
    c1 = caches[1].reshape((dec_batch, WINDOW_KEYS, DILATIONS[1]) + tail)
    c2 = caches[2].reshape((dec_batch, WINDOW_KEYS, DILATIONS[2]) + tail)
    assert DILATIONS[1] == t_new
    kern = functools.partial(_sample_attn_kernel, t_new=t_new)
    class_blk = (SAMPLE_PAIR, WINDOW_KEYS, t_new) + tail
    return pl.pallas_call(
        kern,
        out_shape=jax.ShapeDtypeStruct((dec_batch * t_new, GROUP_W), BF16),
        grid=(dec_batch // SAMPLE_PAIR,),
        in_specs=[pl.BlockSpec((nrow, A_WIDTH), lambda i: (rb0 + i, 0)),
                  pl.BlockSpec((nrow, A_WIDTH), lambda i: (rb0 + i, 1)),
                  pl.BlockSpec((nrow, A_WIDTH), lambda i: (rb0 + i, 0)),
                  pl.BlockSpec((SAMPLE_PAIR, WINDOW_KEYS) + tail, lambda i: (i, 0, 0, 0, 0)),
                  pl.BlockSpec(class_blk, lambda i: (i, 0, 0, 0, 0, 0)),
                  pl.BlockSpec(class_blk, lambda i: (i, 0, 0, 0, 0, 0))],
        out_specs=pl.BlockSpec((nrow, GROUP_W), lambda i: (i, 0)),
        compiler_params=_cparams(("parallel",), vmem=SAMPLE_ATTN_VMEM),
        name="sample_attn",
    )(qk, qk, v, caches[0], c1, c2)


def _cache_roll_kernel(c_ref, n_ref, o_ref):
    last = pl.program_id(0) == pl.num_programs(0) - 1
    o_ref[...] = jnp.where(last, n_ref[...], c_ref[...])


def _roll_cache(cache, new, t_new):
    nb, length = cache.shape[0], cache.shape[1]
    assert length % t_new == 0
    nblk = length // t_new
    blk = (nb, t_new) + cache.shape[2:]
    return pl.pallas_call(
        _cache_roll_kernel,
        out_shape=jax.ShapeDtypeStruct(cache.shape, cache.dtype),
        grid=(nblk,),
        in_specs=[pl.BlockSpec(blk, lambda j: (0, jnp.minimum(j + 1, nblk - 1), 0, 0, 0)),
                  pl.BlockSpec(blk, lambda j: (0, 0, 0, 0, 0))],
        out_specs=pl.BlockSpec(blk, lambda j: (0, j, 0, 0, 0)),
        compiler_params=_cparams(("parallel",)),
        name=f"cache_roll_w{length}",
    )(cache, new)


def _roll_caches(caches, news, t_new):
    return [_roll_cache(c, x.astype(c.dtype), t_new) for c, x in zip(caches, news)]


def _split_bf16(x):
    hi = x.astype(BF16)
    lo = (x - hi.astype(F32)).astype(BF16)
    return hi, lo


def _dot3(a, b, dims):
    ah, al = _split_bf16(a)
    bh, bl = _split_bf16(b)
    dn = (dims, ((), ()))
    out = lax.dot_general(ah, bh, dn, preferred_element_type=F32)
    out = out + lax.dot_general(ah, bl, dn, preferred_element_type=F32)
    return out + lax.dot_general(al, bh, dn, preferred_element_type=F32)


def _cumsum_rows(tri, x):
    hi = x.astype(BF16)
    rem = x - hi.astype(F32)
    mid = rem.astype(BF16)
    lo = (rem - mid.astype(F32)).astype(BF16)
    return (jnp.dot(tri, hi, preferred_element_type=F32) + jnp.dot(tri, mid, preferred_element_type=F32)
            + jnp.dot(tri, lo, preferred_element_type=F32))


_NN = ((1,), (0,))
_NT = ((1,), (1,))
_TN = ((0,), (0,))


def _segsum_mat():
    r = lax.broadcasted_iota(jnp.int32, (LANES, LANES), 0) // HEAD_B
    c = lax.broadcasted_iota(jnp.int32, (LANES, LANES), 1) // HEAD_B
    return (r == c).astype(BF16)


def _segsum(x, ones_bd):
    hi, lo = _split_bf16(x)
    return (jnp.dot(hi, ones_bd, preferred_element_type=F32) + jnp.dot(lo, ones_bd, preferred_element_type=F32))


def _rwkv_prep_kernel(k_ref, wl_ref, al_ref, gl_ref, w2_ref, a2_ref, g2_ref, vec_ref,
                      lw_ref, km_ref, na_ref, nb_ref, g_ref):
    w0, a0, k_k, k_a = vec_ref[0:1, :], vec_ref[1:2, :], vec_ref[2:3, :], vec_ref[3:4, :]
    w_lin = w0 + jnp.dot(jnp.tanh(wl_ref[...]).astype(BF16), w2_ref[...].astype(BF16), preferred_element_type=F32)
    w_log = -jax.nn.softplus(-w_lin) - 0.5
    lw_ref[...] = -jnp.exp(w_log)
    a = jax.nn.sigmoid(a0 + jnp.dot(al_ref[...].astype(BF16), a2_ref[...].astype(BF16), preferred_element_type=F32))
    g_ref[...] = jnp.dot(jax.nn.sigmoid(gl_ref[...]).astype(BF16), g2_ref[...].astype(BF16), preferred_element_type=F32)
    k = k_ref[...]
    km_ref[...] = k * (1.0 + (a - 1.0) * k_a)
    kk = k * k_k
    ones_bd = _segsum_mat()
    for s in range(B_WIDTH // LANES):
        sl = slice(s * LANES, (s + 1) * LANES)
        kks = kk[:, sl]
        nrm = jnp.maximum(jnp.sqrt(_segsum(kks * kks, ones_bd)), 1e-12)
        kkn = kks / nrm
        na_ref[:, sl] = -kkn
        nb_ref[:, sl] = kkn * a[:, sl]


def _rwkv_prep(k, w_low, a_low, g_low, w2, a2, g2_pad, vecs, tm=256):
    n = k.shape[0]
    wide = pl.BlockSpec((tm, B_WIDTH), lambda i: (i, 0))
    full = lambda shape: pl.BlockSpec(shape, lambda i: (0, 0))
    sds = jax.ShapeDtypeStruct((n, B_WIDTH), F32)
    return pl.pallas_call(
        _rwkv_prep_kernel,
        out_shape=(sds,) * 5,
        grid=(n // tm,),
        in_specs=[wide, pl.BlockSpec((tm, DECAY_LORA), lambda i: (i, 0)), pl.BlockSpec((tm, AAA_LORA), lambda i: (i, 0)),
                  pl.BlockSpec((tm, GATE_LORA_PAD), lambda i: (i, 0)),
                  full((DECAY_LORA, B_WIDTH)), full((AAA_LORA, B_WIDTH)), full((GATE_LORA_PAD, B_WIDTH)),
                  full((4, B_WIDTH))],
        out_specs=(wide,) * 5,
        compiler_params=_cparams(("parallel",)),
        name="rwkv_prep",
    )(k, w_low, a_low, g_low, w2, a2, g2_pad, vecs)


def _each(fn, *lists):
    return [fn(*args) for args in zip(*lists)]


def _dot1(a, b, dims):
    return lax.dot_general(a.astype(BF16), b.astype(BF16), (dims, ((), ())), preferred_element_type=F32)


def _mm_each(xs, ys, dims):
    return _each(lambda x, y: _dot1(x, y, dims), xs, ys)


def _tri_inverse(a_bds, c):
    n = 2 * c
    r = lax.broadcasted_iota(jnp.int32, (n, n), 0)
    q = lax.broadcasted_iota(jnp.int32, (n, n), 1)
    eye = (r == q).astype(F32)
    base = 8
    in_base = r // base == q // base
    a0 = [jnp.where(in_base, a, 0.0) for a in a_bds]
    a2 = _mm_each(a0, a0, _NN)
    a4 = _mm_each(a2, a2, _NN)
    x = [eye + a for a in a0]
    x = _each(jnp.add, x, _mm_each(x, a2, _NN))
    x = _each(jnp.add, x, _mm_each(x, a4, _NN))
    size = base
    while size < c:
        below = jnp.logical_and(r // (2 * size) == q // (2 * size), r // size == q // size + 1)
        off = [jnp.where(below, a, 0.0) for a in a_bds]
        x = _each(jnp.add, x, _mm_each(_mm_each(x, off, _NN), x, _NN))
        size *= 2
    return x


def _rwkv_chunk(chains, c):
    n = 2 * c
    lane = lax.broadcasted_iota(jnp.int32, (c, LANES), 1)
    m0 = (lane < HEAD_B).astype(F32)
    m1 = 1.0 - m0
    stack = lambda x: jnp.concatenate([x * m0, x * m1], axis=0)
    r, lw, k, v, a, b, s_bd = (list(col) for col in zip(*chains))

    tr = lax.broadcasted_iota(jnp.int32, (c, c), 0)
    tc = lax.broadcasted_iota(jnp.int32, (c, c), 1)
    tri = (tr >= tc).astype(BF16)
    cl = [_cumsum_rows(tri, x) for x in lw]
    cl_end = [x[c - 1:c, :] for x in cl]
    r_t = _each(lambda x, d: stack(x * jnp.exp(d)), r, cl)
    a_t = _each(lambda x, d, w: stack(x * jnp.exp(d - w)), a, cl, lw)
    b_t = _each(lambda x, d: stack(x * jnp.exp(-d)), b, cl)
    k_t = _each(lambda x, d: stack(x * jnp.exp(-d)), k, cl)
    bk_e = _each(lambda x, y, d, e: jnp.concatenate([stack(x * jnp.exp(e - d)), stack(y * jnp.exp(e - d))], axis=0),
                 b, k, cl, cl_end)
    v_s = [stack(x) for x in v]

    rr = lax.broadcasted_iota(jnp.int32, (n, n), 0)
    qq = lax.broadcasted_iota(jnp.int32, (n, n), 1)
    same = rr // c == qq // c
    strict = jnp.logical_and(same, rr > qq)
    incl = jnp.logical_and(same, rr >= qq)
    ar = _each(lambda x, y: jnp.concatenate([x, y], axis=0), a_t, r_t)
    gb = _mm_each(ar, b_t, _NT)
    gk = _mm_each(ar, k_t, _NT)
    ab = [jnp.where(strict, g[:n], 0.0) for g in gb]
    rb = [jnp.where(incl, g[n:], 0.0) for g in gb]
    ak = [jnp.where(strict, g[:n], 0.0) for g in gk]
    rk = [jnp.where(incl, g[n:], 0.0) for g in gk]

    us = _mm_each(ar, s_bd, _NT)
    akv = _mm_each(ak, v_s, _NN)
    rkv = _mm_each(rk, v_s, _NN)
    t_inv = _tri_inverse(ab, c)
    u = _mm_each(t_inv, _each(lambda x, y: x[:n] + y, us, akv), _NN)
    rbu = _mm_each(rb, u, _NN)
    y_s = _each(lambda x, y, z: x[n:] + y + z, us, rbu, rkv)
    upd = _mm_each(_each(lambda x, y: jnp.concatenate([x, y], axis=0), u, v_s), bk_e, _TN)
    s_new = _each(lambda s, e, d: s * jnp.exp(e) + d, s_bd, cl_end, upd)
    return [(y[:c] + y[c:], s) for y, s in zip(y_s, s_new)]


RWKV_PAIRS = 16


def _rwkv_prompt_kernel(r_ref, lw_ref, k_ref, v_ref, a_ref, b_ref, y_ref, s_ref, s_acc):
    ci = pl.program_id(2)

    @pl.when(ci == 0)
    def _():
        s_acc[...] = jnp.zeros_like(s_acc)

    lanes = [slice(p * LANES, (p + 1) * LANES) for p in range(RWKV_PAIRS)]
    chains = [(r_ref[:, sl], lw_ref[:, sl], k_ref[:, sl], v_ref[:, sl], a_ref[:, sl], b_ref[:, sl], s_acc[p])
              for p, sl in enumerate(lanes)]
    res = _rwkv_chunk(chains, CHUNK)
    for p, sl in enumerate(lanes):
        y_ref[:, sl] = res[p][0]
        s_acc[p] = res[p][1]

    @pl.when(ci == pl.num_programs(2) - 1)
    def _():
        for p in range(RWKV_PAIRS):
            _store_pair_state(s_ref, p, s_acc[p])


def _pair_state(s_ref, p):
    zero = jnp.zeros((HEAD_B, HEAD_B), F32)
    top = jnp.concatenate([s_ref[2 * p], zero], axis=1)
    bot = jnp.concatenate([zero, s_ref[2 * p + 1]], axis=1)
    return jnp.concatenate([top, bot], axis=0)


def _store_pair_state(s_ref, p, s_bd):
    s_ref[2 * p] = s_bd[:HEAD_B, :HEAD_B]
    s_ref[2 * p + 1] = s_bd[HEAD_B:, HEAD_B:]


def _rwkv_prompt(r, lw, k, v, a, b, batch, seq):
    npair = B_WIDTH // LANES
    nchunk = seq // CHUNK
    wide = RWKV_PAIRS * LANES
    spec = pl.BlockSpec((CHUNK, wide), lambda bi, p, ci: (bi * nchunk + ci, p))
    return pl.pallas_call(
        _rwkv_prompt_kernel,
        out_shape=(jax.ShapeDtypeStruct((batch * seq, B_WIDTH), F32),
                   jax.ShapeDtypeStruct((batch, 2 * npair, HEAD_B, HEAD_B), F32)),
        grid=(batch, npair // RWKV_PAIRS, nchunk),
        in_specs=[spec] * 6,
        out_specs=(spec, pl.BlockSpec((None, 2 * RWKV_PAIRS, HEAD_B, HEAD_B), lambda bi, p, ci: (bi, p, 0, 0))),
        scratch_shapes=[pltpu.VMEM((RWKV_PAIRS, LANES, LANES), F32)],
        compiler_params=_cparams(("parallel", "parallel", "arbitrary")),
        name="rwkv_prompt",
    )(r, lw, k, v, a, b)


def _rwkv_sample_kernel(r_ref, lw_ref, k_ref, v_ref, a_ref, b_ref, s0_ref, y_ref, s_ref, *, t_new):
    c = SAMPLE_CHUNK
    row = lax.broadcasted_iota(jnp.int32, (c, LANES), 0)
    def sel(ref, sl, e):
        x = ref[:, sl]
        if e:
            x = pltpu.roll(x, c - e * t_new, 0)
        return jnp.where(row < t_new, x, 0.0)

    ids = [(p, e) for p in range(RWKV_PAIRS) for e in range(SAMPLE_PAIR)]
    chains = []
    for p, e in ids:
        sl = slice(p * LANES, (p + 1) * LANES)
        chains.append(tuple(sel(ref, sl, e) for ref in (r_ref, lw_ref, k_ref, v_ref, a_ref, b_ref))
                      + (_pair_state(s0_ref.at[e], p),))
    res = dict(zip(ids, _rwkv_chunk(chains, c)))
    for p in range(RWKV_PAIRS):
        y_all = res[(p, 0)][0]
        for e in range(1, SAMPLE_PAIR):
            y_all = jnp.where(row // t_new == e, pltpu.roll(res[(p, e)][0], e * t_new, 0), y_all)
        y_ref[:, p * LANES:(p + 1) * LANES] = y_all
        for e in range(SAMPLE_PAIR):
            _store_pair_state(s_ref.at[e], p, res[(p, e)][1])


def _rwkv_sample(r, lw, k, v, a, b, s0, row0, dec_batch, t_new):
    npair = B_WIDTH // LANES
    nrow = SAMPLE_PAIR * t_new
    rb0 = row0 // nrow
    wide = RWKV_PAIRS * LANES
    spec = pl.BlockSpec((nrow, wide), lambda i, p: (rb0 + i, p))
    s_spec = pl.BlockSpec((SAMPLE_PAIR, 2 * RWKV_PAIRS, HEAD_B, HEAD_B), lambda i, p: (i, p, 0, 0))
    return pl.pallas_call(
        functools.partial(_rwkv_sample_kernel, t_new=t_new),
        out_shape=(jax.ShapeDtypeStruct((dec_batch * t_new, B_WIDTH), F32),
                   jax.ShapeDtypeStruct(s0.shape, F32)),
        grid=(dec_batch // SAMPLE_PAIR, npair // RWKV_PAIRS),
        in_specs=[spec] * 6 + [s_spec],
        out_specs=(pl.BlockSpec((nrow, wide), lambda i, p: (i, p)), s_spec),
        compiler_params=_cparams(("parallel", "parallel")),
        name="rwkv_sample",
    )(r, lw, k, v, a, b, s0)


def _rwkv_post_kernel(y_ref, r_ref, k_ref, v_ref, g_ref, vec_ref, o_ref):
    ones_bd = _segsum_mat()
    inv = 1.0 / HEAD_B
    for s in range(B_WIDTH // LANES):
        sl = slice(s * LANES, (s + 1) * LANES)
        y = y_ref[:, sl]
        mean = _segsum(y, ones_bd) * inv
        dlt = y - mean
        var = _segsum(dlt * dlt, ones_bd) * inv
        yn = dlt * lax.rsqrt(var + GN_EPS) * vec_ref[0:1, sl] + vec_ref[1:2, sl]
        bonus = _segsum(r_ref[:, sl] * k_ref[:, sl] * vec_ref[2:3, sl], ones_bd) * v_ref[:, sl]
        o_ref[:, sl] = ((yn + bonus) * g_ref[:, sl]).astype(o_ref.dtype)


def _rwkv_post(y, r, k_mod, v, g, vecs, tm=256):
    n = y.shape[0]
    wide = pl.BlockSpec((tm, B_WIDTH), lambda i: (i, 0))
    return pl.pallas_call(
        _rwkv_post_kernel,
        out_shape=jax.ShapeDtypeStruct((n, B_WIDTH), BF16),
        grid=(n // tm,),
        in_specs=[wide] * 5 + [pl.BlockSpec((3, B_WIDTH), lambda i: (0, 0))],
        out_specs=wide,
        compiler_params=_cparams(("parallel",)),
        name="rwkv_post",
    )(y, r, k_mod, v, g, vecs)


def _read_kernel(ao_ref, bo_ref, wa_ref, wb_ref, ga_ref, gb_ref, o_ref):
    ra = jnp.dot(ao_ref[...], wa_ref[...].astype(BF16), preferred_element_type=F32)
    rb = jnp.dot(bo_ref[...], wb_ref[...].astype(BF16), preferred_element_type=F32)
    o_ref[...] = (ga_ref[...] * ra + gb_ref[...] * rb).astype(o_ref.dtype)


def _gated_read(a_out, b_out, w_read_a, w_read_b, gates, tm, tn=512):
    n = a_out.shape[0]
    ncol = D_MODEL // tn
    return pl.pallas_call(
        _read_kernel,
        out_shape=jax.ShapeDtypeStruct((n, D_MODEL), BF16),
        grid=(ncol, n // tm),
        in_specs=[pl.BlockSpec((tm, GROUP_W), lambda j, i: (i, 0)),
                  pl.BlockSpec((tm, B_WIDTH), lambda j, i: (i, 0)),
                  pl.BlockSpec((GROUP_W, tn), lambda j, i: (0, j)),
                  pl.BlockSpec((B_WIDTH, tn), lambda j, i: (0, j)),
                  pl.BlockSpec((tm, tn), lambda j, i: (i, j)),
                  pl.BlockSpec((tm, tn), lambda j, i: (i, j + ncol))],
        out_specs=pl.BlockSpec((tm, tn), lambda j, i: (i, j)),
        compiler_params=_cparams(("parallel", "parallel")),
        name="gated_read",
    )(a_out, b_out, w_read_a, w_read_b, gates, gates)


def _residual_epilogue(acc, x_ref):
    return x_ref[...] + acc


def _pack_halves(h):
    half = h.shape[1] // 2
    top = pltpu.bitcast(h[:, :half].astype(BF16).astype(F32), jnp.uint32)
    bot = pltpu.bitcast(h[:, half:].astype(BF16).astype(F32), jnp.uint32)
    return top | (bot >> 16)


def _unpack_halves(p):
    top = pltpu.bitcast(p & jnp.uint32(0xFFFF0000), F32).astype(BF16)
    bot = pltpu.bitcast(p << 16, F32).astype(BF16)
    return top, bot


def _ffn_norm_router_kernel(u_ref, g_ref, wr_ref, hb_ref, lg_ref):
    x = u_ref[...]
    ms = jnp.mean(x * x, axis=-1, keepdims=True)
    h = x * lax.rsqrt(ms + NORM_EPS) * g_ref[...]
    hb_ref[...] = _pack_halves(h)
    lg_ref[...] = _dot3(h, wr_ref[...], _NN)


def _ffn_norm_router(u, gain, w_router, tm=256):
    n, d = u.shape
    return pl.pallas_call(
        _ffn_norm_router_kernel,
        out_shape=(jax.ShapeDtypeStruct((n, d // 2), jnp.uint32), jax.ShapeDtypeStruct((n, ROUTER_PAD), F32)),
        grid=(n // tm,),
        in_specs=[pl.BlockSpec((tm, d), lambda i: (i, 0)), pl.BlockSpec((1, d), lambda i: (0, 0)),
                  pl.BlockSpec((d, ROUTER_PAD), lambda i: (0, 0))],
        out_specs=(pl.BlockSpec((tm, d // 2), lambda i: (i, 0)), pl.BlockSpec((tm, ROUTER_PAD), lambda i: (i, 0))),
        compiler_params=_cparams(("parallel",)),
        name="ffn_norm_router",
    )(u, gain.reshape(1, d), w_router)


def _route(logits, group_b, expert_b):
    n = logits.shape[0]
    group_logits = logits[:, :N_EXPERT_GROUPS] + group_b.astype(F32)
    group = jnp.argmax(group_logits, axis=-1).astype(jnp.int32)
    p_group = jnp.take_along_axis(jax.nn.softmax(group_logits, axis=-1), group[:, None], axis=-1)
    expert_logits = (logits[:, N_EXPERT_GROUPS:N_EXPERT_GROUPS + N_EXPERTS] + expert_b.astype(F32)).reshape(
        n, N_EXPERT_GROUPS, EXPERTS_PER_GROUP)
    in_group = jnp.take_along_axis(expert_logits, group[:, None, None], axis=1)[:, 0]
    top_val, top_idx = lax.top_k(in_group, TOP_K)
    gate = p_group * jax.nn.softmax(top_val, axis=-1)
    expert_id = group[:, None] * EXPERTS_PER_GROUP + top_idx.astype(jnp.int32)
    return expert_id, gate


FFN_CHUNKS = 1
FFN_VMEM = 56 * 1024 * 1024


def _moe_plan(expert_id):
    n_assign = expert_id.size
    n_blocks = n_assign // ROW_BLOCK + N_EXPERTS
    n_rows = n_blocks * ROW_BLOCK
    max_items = FFN_CHUNKS * n_blocks
    i32 = jnp.int32
    e_flat = expert_id.reshape(n_assign)
    onehot = (e_flat[:, None] == jnp.arange(N_EXPERTS, dtype=i32)[None, :]).astype(i32)
    running = jnp.cumsum(onehot, axis=0)
    rank = jnp.sum(onehot * (running - 1), axis=1)
    counts = running[-1]
    nblk = (counts + ROW_BLOCK - 1) // ROW_BLOCK
    blk_start = jnp.cumsum(nblk) - nblk
    pos = (blk_start[e_flat] * ROW_BLOCK + rank).astype(i32)
    row_src = jnp.zeros((n_rows,), i32).at[pos].set(jnp.arange(n_assign, dtype=i32) // TOP_K)
    item_cnt = FFN_CHUNKS * nblk
    item_end = jnp.cumsum(item_cnt)
    item_start = item_end - item_cnt
    total = item_end[-1]
    idx = jnp.arange(max_items, dtype=i32)
    ic = jnp.minimum(idx, total - 1)
    it_e = jnp.minimum(jnp.sum((item_end[None, :] <= ic[:, None]).astype(i32), axis=1), N_EXPERTS - 1)
    within = ic - item_start[it_e]
    nb = jnp.maximum(nblk[it_e], 1)
    it_wc = within // nb
    valid = idx < total
    first = jnp.logical_and(valid, within % nb == 0)
    tail = idx - total
    it_blk = jnp.where(valid, blk_start[it_e] + within % nb, total // FFN_CHUNKS + tail // FFN_CHUNKS)
    it_c = jnp.where(valid, it_wc, tail % FFN_CHUNKS)
    return dict(pos=pos, row_src=row_src, it_e=it_e, it_wc=it_wc.astype(i32), it_c=it_c.astype(i32),
                it_blk=it_blk.astype(i32), it_valid=valid.astype(i32), it_first=first.astype(i32),
                n_rows=n_rows, max_items=max_items)


GATHER_ROWS = 128


def _dispatch_kernel(src_ref, x_hbm, o_ref, buf, sem):
    i = pl.program_id(0)
    slot = i % 2

    def row_copy(slot_, j, src_row):
        return pltpu.make_async_copy(x_hbm.at[pl.ds(src_row, 1)], buf.at[slot_, pl.ds(j, 1)], sem.at[slot_])

    def issue(block, slot_):
        base = block * GATHER_ROWS

        def body(j, carry):
            row_copy(slot_, j, src_ref[base + j]).start()
            return carry

        lax.fori_loop(0, GATHER_ROWS, body, 0, unroll=8)

    @pl.when(i == 0)
    def _():
        issue(0, 0)

    @pl.when(i + 1 < pl.num_programs(0))
    def _():
        issue(i + 1, 1 - slot)

    for j in range(GATHER_ROWS):
        row_copy(slot, j, 0).wait()
    o_ref[...] = buf[slot]


def _dispatch_rows(x, row_src, n_rows):
    d = x.shape[1]
    return pl.pallas_call(
        _dispatch_kernel,
        out_shape=jax.ShapeDtypeStruct((n_rows, d), x.dtype),
        grid_spec=pltpu.PrefetchScalarGridSpec(
            num_scalar_prefetch=1,
            grid=(n_rows // GATHER_ROWS,),
            in_specs=[pl.BlockSpec(memory_space=pl.ANY)],
            out_specs=pl.BlockSpec((GATHER_ROWS, d), lambda i, src: (i, 0)),
            scratch_shapes=[pltpu.VMEM((2, GATHER_ROWS, d), x.dtype), pltpu.SemaphoreType.DMA((2,))],
        ),
        compiler_params=_cparams(("arbitrary",)),
        name="moe_dispatch",
    )(row_src, x)


def _ffn_kernel(e_ref, wc_ref, c_ref, blk_ref, valid_ref, first_ref, *refs, mode):
    if mode == "up":
        x_ref, g_ref, w_ref, o_ref, w_bf = refs
    else:
        x_ref, w_ref, o_ref, w_bf = refs
    i = pl.program_id(0)

    @pl.when(first_ref[i] == 1)
    def _():
        w_bf[...] = w_ref[...].astype(BF16)

    @pl.when(valid_ref[i] == 0)
    def _():
        o_ref[...] = jnp.zeros_like(o_ref)

    @pl.when(valid_ref[i] == 1)
    def _():
        if mode == "down":
            acc = jnp.dot(x_ref[...], w_bf[...], preferred_element_type=F32)
        else:
            top, bot = _unpack_halves(x_ref[...])
            half = top.shape[1]
            acc = (jnp.dot(top, w_bf[:half, :], preferred_element_type=F32)
                   + jnp.dot(bot, w_bf[half:, :], preferred_element_type=F32))
        if mode == "up":
            acc = jax.nn.silu(g_ref[...]) * acc
        o_ref[...] = acc.astype(o_ref.dtype)


def _expert_matmul(mode, rows, w, plan, extra=None):
    n_rows, row_w = rows.shape
    k, out_w = w.shape[1], w.shape[2]
    prefetch = (plan["it_e"], plan["it_wc"], plan["it_c"], plan["it_blk"], plan["it_valid"], plan["it_first"])
    row_spec = lambda width: pl.BlockSpec((ROW_BLOCK, width), lambda i, e, wc, c, b, v, f: (b[i], 0))
    in_specs = [row_spec(row_w)] + ([row_spec(out_w)] if mode == "up" else [])
    in_specs.append(pl.BlockSpec((None, k, out_w), lambda i, e, wc, c, b, v, f: (e[i], 0, 0)))
    args = (rows,) + ((extra,) if mode == "up" else ()) + (w,)
    return pl.pallas_call(
        functools.partial(_ffn_kernel, mode=mode),
        out_shape=jax.ShapeDtypeStruct((n_rows, out_w), BF16 if mode == "up" else F32),
        grid_spec=pltpu.PrefetchScalarGridSpec(
            num_scalar_prefetch=len(prefetch),
            grid=(plan["max_items"],),
            in_specs=in_specs,
            out_specs=row_spec(out_w),
            scratch_shapes=[pltpu.VMEM((k, out_w), BF16)],
        ),
        compiler_params=_cparams(("arbitrary",), vmem=FFN_VMEM),
        name=f"expert_ffn_{mode}",
    )(*prefetch, *args)


def _expert_ffn(x_sorted, plan, w_gate, w_up, w_down):
    gate = _expert_matmul("gate", x_sorted, w_gate, plan)
    hid = _expert_matmul("up", x_sorted, w_up, plan, extra=gate)
    return _expert_matmul("down", hid, w_down, plan)


COMBINE_ROWS = 64


def _combine_kernel(pos_ref, u_ref, g_ref, y_hbm, o_ref, buf, sem):
    i = pl.program_id(0)
    slot = i % 2

    def row_copy(slot_, s, j, src_row):
        return pltpu.make_async_copy(y_hbm.at[pl.ds(src_row, 1)], buf.at[slot_, s, pl.ds(j, 1)], sem.at[slot_])

    def issue(tile, slot_):
        base = tile * COMBINE_ROWS

        def body(j, carry):
            for s in range(TOP_K):
                row_copy(slot_, s, j, pos_ref[(base + j) * TOP_K + s]).start()
            return carry

        lax.fori_loop(0, COMBINE_ROWS, body, 0, unroll=8)

    @pl.when(i == 0)
    def _():
        issue(0, 0)

    @pl.when(i + 1 < pl.num_programs(0))
    def _():
        issue(i + 1, 1 - slot)

    for j in range(COMBINE_ROWS):
        for s in range(TOP_K):
            row_copy(slot, s, j, 0).wait()
    g = g_ref[...]
    moe = buf[slot, 0] * g[:, 0:1] + buf[slot, 1] * g[:, 1:2]
    o_ref[...] = u_ref[...] + moe


def _combine(u, gate, y_sorted, pos):
    n, d = u.shape
    return pl.pallas_call(
        _combine_kernel,
        out_shape=jax.ShapeDtypeStruct((n, d), F32),
        grid_spec=pltpu.PrefetchScalarGridSpec(
            num_scalar_prefetch=1,
            grid=(n // COMBINE_ROWS,),
            in_specs=[pl.BlockSpec((COMBINE_ROWS, d), lambda i, p: (i, 0)),
                      pl.BlockSpec((COMBINE_ROWS, TOP_K), lambda i, p: (i, 0)),
                      pl.BlockSpec(memory_space=pl.ANY)],
            out_specs=pl.BlockSpec((COMBINE_ROWS, d), lambda i, p: (i, 0)),
            scratch_shapes=[pltpu.VMEM((2, TOP_K, COMBINE_ROWS, d), F32), pltpu.SemaphoreType.DMA((2,))],
        ),
        compiler_params=_cparams(("arbitrary",)),
        name="moe_combine",
    )(pos, u, gate, y_sorted)


def _pick_tm(n):
    for tm in (1088, 1024, 544, 512, 272, 256, 128):
        if n % tm == 0:
            return tm
    raise ValueError(f"row count {n} is not a multiple of 128")


def kernel(x_prompt, x_sample, cache_kv_w128, cache_kv_w512, cache_kv_w2048, state_shift, state_wkv, norm_mix, w_in, q_norm, k_norm, mu_shift, rwkv_w0, rwkv_w2, rwkv_a0, rwkv_a2, rwkv_g2, rwkv_k_k, rwkv_k_a, rwkv_r_k, rwkv_ln_w, rwkv_ln_b, w_read_a, w_read_b, w_o, norm_ffn, router_group_w, router_group_b, router_expert_w, router_expert_b, expert_w_gate, expert_w_up, expert_w_down):
    batch, seq, d = x_prompt.shape
    dec_batch, t_new, _ = x_sample.shape
    n_p, n_s = batch * seq, dec_batch * t_new
    n = n_p + n_s
    past = cache_kv_w2048.shape[1]
    assert d == D_MODEL and seq % (DILATIONS[2] * ROW_BLOCK) == 0 and seq % CHUNK == 0
    assert cache_kv_w128.shape[1] == 128 and cache_kv_w512.shape[1] == 512 and past == 2048
    tm = _pick_tm(n)

    x = jnp.concatenate([x_prompt.reshape(n_p, d), x_sample.reshape(n_s, d)], axis=0)
    h = _rmsnorm(x, norm_mix)
    shift_p = h[seq - 1:n_p:seq]
    shift_s = h[n_p + t_new - 1::t_new]
    first_rows = jnp.broadcast_to(state_shift[:, None].astype(F32), (dec_batch, t_new, d)).reshape(n_s, d)
    hb, mixes = _token_shift_mix(h, first_rows, mu_shift, n_p, seq, t_new)

    positions = jnp.concatenate([jnp.tile(jnp.arange(seq, dtype=jnp.int32), batch),
                                 jnp.tile(past + jnp.arange(t_new, dtype=jnp.int32), dec_batch)])
    rope_c, rope_a, rope_b = _rope_tables(positions)
    gains = jnp.stack([q_norm, k_norm]).reshape(2, 1, HEAD_A).astype(F32)
    tn = GROUP_W
    row_spec = pl.BlockSpec((tm, HEAD_A), lambda j, i: (i, 0))
    w_in_t = w_in.T
    proj = functools.partial(_matmul, w=w_in_t, w_transposed=True, tm=tm)
    qk = proj(hb, col_off=0, width=2 * A_WIDTH, tn=tn, epilogue=_qk_epilogue,
              extra=(gains, rope_c, rope_a, rope_b),
              extra_specs=(pl.BlockSpec((None, 1, HEAD_A), lambda j, i: (j // N_GROUPS, 0, 0)),
                           row_spec, row_spec, row_spec), name="proj_qk")
    v_a = proj(hb, col_off=2 * A_WIDTH, width=A_WIDTH, tn=tn, name="proj_v")
    gates = proj(hb, col_off=3 * A_WIDTH, width=2 * D_MODEL, tn=tn, epilogue=_sigmoid_epilogue, name="proj_gates")
    r_b = proj(mixes, a_sel=0, col_off=X_COLS, width=B_WIDTH, tn=tn, name="proj_r")
    k_b = proj(mixes, a_sel=1, col_off=X_COLS + B_WIDTH, width=B_WIDTH, tn=tn, name="proj_k")
    v_b = proj(mixes, a_sel=2, col_off=X_COLS + 2 * B_WIDTH, width=B_WIDTH, tn=tn, name="proj_vb")
    lora0 = X_COLS + 3 * B_WIDTH
    w_low = proj(mixes, a_sel=3, col_off=lora0, width=DECAY_LORA, tn=DECAY_LORA, name="proj_wlow")
    a_low = proj(mixes, a_sel=4, col_off=lora0 + DECAY_LORA, width=AAA_LORA, tn=AAA_LORA, name="proj_alow")
    g_low = proj(mixes, a_sel=5, col_off=lora0 + DECAY_LORA + AAA_LORA, width=GATE_LORA_PAD, tn=DECAY_LORA,
                 epilogue=functools.partial(_valid_cols_epilogue, valid=GATE_LORA), name="proj_glow")

    a_out_p = _prompt_attention(qk, v_a, batch, seq)
    caches = (cache_kv_w128, cache_kv_w512, cache_kv_w2048)
    a_out_s = _sample_attention(qk, v_a, caches, n_p, dec_batch, t_new)
    a_out = jnp.concatenate([a_out_p, a_out_s], axis=0)

    def kv_rows(g, rows):
        kk = qk[rows, A_WIDTH + g * GROUP_W:A_WIDTH + (g + 1) * GROUP_W]
        vv = v_a[rows, g * GROUP_W:(g + 1) * GROUP_W]
        return jnp.stack([kk.reshape(-1, HEADS_PER_GROUP, HEAD_A), vv.reshape(-1, HEADS_PER_GROUP, HEAD_A)], axis=1)

    kv_prompt = []
    for g in range(N_GROUPS):
        keep = min(caches[g].shape[1], seq)
        full = kv_rows(g, slice(0, n_p)).reshape(batch, seq, 2, HEADS_PER_GROUP, HEAD_A)
        kv_prompt.append(full[:, seq - keep:])
    news = [kv_rows(g, slice(n_p, n)).reshape(dec_batch, t_new, 2, HEADS_PER_GROUP, HEAD_A) for g in range(N_GROUPS)]
    kv_sample = _roll_caches(caches, news, t_new)

    g2_pad = jnp.pad(rwkv_g2, ((0, GATE_LORA_PAD - GATE_LORA), (0, 0)))
    vec_prep = jnp.stack([rwkv_w0, rwkv_a0, rwkv_k_k, rwkv_k_a]).astype(F32)
    lw, k_mod, neg_kk, kk_a, g_out = _rwkv_prep(k_b, w_low, a_low, g_low, rwkv_w2, rwkv_a2, g2_pad, vec_prep)
    y_p, s_p = _rwkv_prompt(r_b, lw, k_mod, v_b, neg_kk, kk_a, batch, seq)
    y_s, s_s = _rwkv_sample(r_b, lw, k_mod, v_b, neg_kk, kk_a, state_wkv.astype(F32), n_p, dec_batch, t_new)
    y_b = jnp.concatenate([y_p, y_s], axis=0)
    vec_post = jnp.stack([rwkv_ln_w, rwkv_ln_b, rwkv_r_k.reshape(B_WIDTH)]).astype(F32)
    b_out = _rwkv_post(y_b, r_b, k_mod, v_b, g_out, vec_post)

    merged = _gated_read(a_out, b_out, w_read_a, w_read_b, gates, tm)
    u = _matmul(merged, w_o, tm=tm, tn=tn, epilogue=_residual_epilogue, extra=(x,),
                extra_specs=(pl.BlockSpec((tm, tn), lambda j, i: (i, j)),), name="proj_out")
    w_router = jnp.pad(jnp.concatenate([router_group_w, router_expert_w], axis=1),
                       ((0, 0), (0, ROUTER_PAD - N_EXPERT_GROUPS - N_EXPERTS)))
    hn, logits = _ffn_norm_router(u, norm_ffn, w_router)
    expert_id, gate = _route(logits, router_group_b, router_expert_b)
    plan = _moe_plan(expert_id)
    x_sorted = _dispatch_rows(hn, plan["row_src"], plan["n_rows"])
    y_sorted = _expert_ffn(x_sorted, plan, expert_w_gate, expert_w_up, expert_w_down)
    y = _combine(u, gate, y_sorted, plan["pos"])

    return (y[:n_p].reshape(batch, seq, d), y[n_p:].reshape(dec_batch, t_new, d),
            kv_prompt[0], kv_prompt[1], kv_prompt[2], shift_p, s_p.astype(state_wkv.dtype),
            kv_sample[0], kv_sample[1], kv_sample[2], shift_s, s_s.astype(state_wkv.dtype))
```

```python
import functools

import jax
import jax.numpy as jnp
from jax import lax
from jax.experimental import pallas as pl
from jax.experimental.pallas import tpu as pltpu

F32 = jnp.float32
BF16 = jnp.bfloat16

LANES = 128
VMEM_LIMIT = 48 * 1024 * 1024

D_MODEL = 4096
HEAD_A = 128
N_GROUPS = 3
HEADS_PER_GROUP = 4
GROUP_W = HEADS_PER_GROUP * HEAD_A
A_WIDTH = N_GROUPS * GROUP_W
DILATIONS = (1, 4, 16)
WINDOW_KEYS = 128
ROPE_THETA = 500000.0
ROPE_DIM = HEAD_A // 4
ROPE_HALF = ROPE_DIM // 2
HEAD_B = 64
B_WIDTH = D_MODEL // 2
DECAY_LORA = 128
AAA_LORA = 128
GATE_LORA = 480
GATE_LORA_PAD = 512
GN_EPS = 64e-5
NORM_EPS = 1e-6
X_COLS = 3 * A_WIDTH + 2 * D_MODEL
N_EXPERT_GROUPS = 8
EXPERTS_PER_GROUP = 8
N_EXPERTS = 64
TOP_K = 2
D_EXPERT = 1024
ROW_BLOCK = 128
ROUTER_PAD = 128
CHUNK = 64
SAMPLE_CHUNK = 8


def _cparams(sem, vmem=VMEM_LIMIT):
    return pltpu.CompilerParams(dimension_semantics=sem, vmem_limit_bytes=vmem)


def _rmsnorm_kernel(x_ref, g_ref, o_ref):
    x = x_ref[...]
    ms = jnp.mean(x * x, axis=-1, keepdims=True)
    o_ref[...] = x * lax.rsqrt(ms + NORM_EPS) * g_ref[...]


def _rmsnorm(x, gain, tm=256):
    n, d = x.shape
    return pl.pallas_call(
        _rmsnorm_kernel,
        out_shape=jax.ShapeDtypeStruct((n, d), F32),
        grid=(n // tm,),
        in_specs=[pl.BlockSpec((tm, d), lambda i: (i, 0)), pl.BlockSpec((1, d), lambda i: (0, 0))],
        out_specs=pl.BlockSpec((tm, d), lambda i: (i, 0)),
        compiler_params=_cparams(("parallel",)),
        name="rmsnorm",
    )(x, gain.reshape(1, d))


SUBLANES = 8


def _mix_kernel(h_ref, before_ref, first_ref, mu_ref, hb_ref, m_ref, *, n_prompt, seq, t_new):
    i = pl.program_id(0)
    tm = h_ref.shape[0]
    h = h_ref[...]
    row = lax.broadcasted_iota(jnp.int32, (tm, 1), 0)
    prev = jnp.where(row == 0, before_ref[SUBLANES - 1:SUBLANES, :], pltpu.roll(h, 1, 0))
    is_sample = i * tm >= n_prompt
    sample_start = jnp.where(row % t_new == 0, 1, 0)
    prompt_start = jnp.where((i * tm + row) % seq == 0, 1, 0)
    starts = jnp.where(is_sample, sample_start, prompt_start) == 1
    first = jnp.where(is_sample, first_ref[...], 0.0)
    xx = jnp.where(starts, first, prev) - h
    hb_ref[...] = h.astype(BF16)
    for j in range(6):
        m_ref[j] = (h + xx * mu_ref[j:j + 1, :]).astype(BF16)


def _token_shift_mix(h, first_rows, mu, n_prompt, seq, t_new, tm=128):
    n, d = h.shape
    assert n_prompt % tm == 0 and seq % tm == 0 and tm % t_new == 0 and tm % SUBLANES == 0
    per = tm // SUBLANES
    kern = functools.partial(_mix_kernel, n_prompt=n_prompt, seq=seq, t_new=t_new)
    return pl.pallas_call(
        kern,
        out_shape=(jax.ShapeDtypeStruct((n, d), BF16), jax.ShapeDtypeStruct((6, n, d), BF16)),
        grid=(n // tm,),
        in_specs=[pl.BlockSpec((tm, d), lambda i: (i, 0)),
                  pl.BlockSpec((SUBLANES, d), lambda i: (jnp.maximum(i * per - 1, 0), 0)),
                  pl.BlockSpec((tm, d), lambda i: (jnp.maximum(i - n_prompt // tm, 0), 0)),
                  pl.BlockSpec((6, d), lambda i: (0, 0))],
        out_specs=(pl.BlockSpec((tm, d), lambda i: (i, 0)), pl.BlockSpec((6, tm, d), lambda i: (0, i, 0))),
        compiler_params=_cparams(("parallel",)),
        name="token_shift_mix",
    )(h, h, first_rows, mu)


def _mm_kernel(*refs, n_extra, epilogue, w_transposed):
    a_ref, w_ref = refs[0], refs[1]
    extra = refs[2:2 + n_extra]
    o_ref = refs[2 + n_extra]
    dims = _NT if w_transposed else _NN
    acc = lax.dot_general(a_ref[...], w_ref[...].astype(BF16), (dims, ((), ())), preferred_element_type=F32)
    o_ref[...] = epilogue(acc, *extra).astype(o_ref.dtype)


def _matmul(a, w, *, col_off=0, width=None, tm, tn, epilogue=None, extra=(), extra_specs=(),
            out_dtype=F32, a_sel=None, w_transposed=False, name="matmul"):
    if a_sel is None:
        n, k = a.shape
        a_spec = pl.BlockSpec((tm, k), lambda j, i: (i, 0))
    else:
        _, n, k = a.shape
        a_spec = pl.BlockSpec((None, tm, k), lambda j, i: (a_sel, i, 0))
    n_cols = w.shape[0] if w_transposed else w.shape[1]
    width = n_cols - col_off if width is None else width
    assert col_off % tn == 0 and width % tn == 0 and n % tm == 0
    off = col_off // tn
    if epilogue is None:
        epilogue = lambda acc: acc
    kern = functools.partial(_mm_kernel, n_extra=len(extra), epilogue=epilogue, w_transposed=w_transposed)
    w_spec = (pl.BlockSpec((tn, k), lambda j, i: (j + off, 0)) if w_transposed
              else pl.BlockSpec((k, tn), lambda j, i: (0, j + off)))
    return pl.pallas_call(
        kern,
        out_shape=jax.ShapeDtypeStruct((n, width), out_dtype),
        grid=(width // tn, n // tm),
        in_specs=[a_spec, w_spec] + list(extra_specs),
        out_specs=pl.BlockSpec((tm, tn), lambda j, i: (i, j)),
        compiler_params=_cparams(("parallel", "parallel")),
        name=name,
    )(a, w, *extra)


def _qk_epilogue(acc, gain_ref, c_ref, a_ref, b_ref):
    c, a, b = c_ref[...], a_ref[...], b_ref[...]
    g = gain_ref[...]
    outs = []
    for hh in range(HEADS_PER_GROUP):
        x = acc[:, hh * HEAD_A:(hh + 1) * HEAD_A]
        ms = jnp.mean(x * x, axis=-1, keepdims=True)
        y = x * lax.rsqrt(ms + NORM_EPS) * g
        y = y * c + pltpu.roll(y, HEAD_A - ROPE_HALF, 1) * a + pltpu.roll(y, ROPE_HALF, 1) * b
        outs.append(y)
    return jnp.concatenate(outs, axis=1)


def _sigmoid_epilogue(acc):
    return jax.nn.sigmoid(acc)


def _valid_cols_epilogue(acc, *, valid):
    col = pl.program_id(0) * acc.shape[1] + lax.broadcasted_iota(jnp.int32, acc.shape, 1)
    return jnp.where(col < valid, acc, 0.0)


def _rope_tables(positions):
    n = positions.shape[0]
    inv_freq = ROPE_THETA ** (-jnp.arange(ROPE_HALF, dtype=F32) / ROPE_HALF)
    ang = positions.astype(F32)[:, None] * inv_freq[None, :]
    cos, sin = jnp.cos(ang), jnp.sin(ang)
    zeros = lambda w: jnp.zeros((n, w), F32)
    c = jnp.concatenate([cos, cos, jnp.ones((n, HEAD_A - ROPE_DIM), F32)], axis=1)
    a = jnp.concatenate([-sin, zeros(HEAD_A - ROPE_HALF)], axis=1)
    b = jnp.concatenate([zeros(ROPE_HALF), sin, zeros(HEAD_A - ROPE_DIM)], axis=1)
    return c, a, b


def _softmax_parts(s):
    m = jnp.max(s, axis=-1, keepdims=True)
    e = jnp.exp(s - m)
    denom = jnp.sum(e, axis=-1, keepdims=True)
    return e / denom, m + jnp.log(denom)


def _merge_three(outs, lses):
    m = jnp.maximum(jnp.maximum(lses[0], lses[1]), lses[2])
    es = [jnp.exp(l - m) for l in lses]
    tot = es[0] + es[1] + es[2]
    return (es[0] / tot) * outs[0] + (es[1] / tot) * outs[1] + (es[2] / tot) * outs[2]


def _prompt_attn_kernel(*refs, seq):
    ins, out_ref, o_s, l_s = refs[:3 * N_GROUPS], refs[3 * N_GROUPS], refs[3 * N_GROUPS + 1], refs[3 * N_GROUPS + 2]
    rows = lax.broadcasted_iota(jnp.int32, (ROW_BLOCK, ROW_BLOCK), 0)
    cols = lax.broadcasted_iota(jnp.int32, (ROW_BLOCK, ROW_BLOCK), 1)
    cur_ok = cols <= rows
    prev_ok = cols >= rows
    scale = HEAD_A ** -0.5
    nt = (((1,), (1,)), ((), ()))
    for g, d in enumerate(DILATIONS):
        q_ref, k_ref, v_ref = ins[3 * g:3 * g + 3]
        for c in range(d):
            for qb in range(seq // d // ROW_BLOCK):
                def class_rows(blk):
                    return pl.ds(blk * ROW_BLOCK, ROW_BLOCK) if d == 1 else pl.ds(c + blk * ROW_BLOCK * d, ROW_BLOCK, stride=d)

                sl = class_rows(qb)
                q = q_ref[sl, :].astype(BF16)
                v = v_ref[sl, :].astype(BF16)
                s = lax.dot_general(q, k_ref[sl, :].astype(BF16), nt, preferred_element_type=F32) * scale
                s = jnp.where(cur_ok, s, -jnp.inf)
                if qb > 0:
                    before = class_rows(qb - 1)
                    s_prev = lax.dot_general(q, k_ref[before, :].astype(BF16), nt, preferred_element_type=F32) * scale
                    s = jnp.concatenate([jnp.where(prev_ok, s_prev, -jnp.inf), s], axis=1)
                    v = jnp.concatenate([v_ref[before, :].astype(BF16), v], axis=0)
                p, lse = _softmax_parts(s)
                o_s[g, sl, :] = jnp.dot(p.astype(BF16), v, preferred_element_type=F32)
                l_s[g, sl, :] = jnp.broadcast_to(lse, (ROW_BLOCK, HEAD_A))
    out_ref[...] = _merge_three([o_s[g] for g in range(N_GROUPS)],
                                [l_s[g] for g in range(N_GROUPS)]).astype(out_ref.dtype)


def _prompt_attention(qk, v, batch, seq):
    heads = N_GROUPS * HEADS_PER_GROUP
    specs, args = [], []
    for g in range(N_GROUPS):
        specs += [pl.BlockSpec((seq, HEAD_A), lambda b, hh, g=g: (b, g * HEADS_PER_GROUP + hh)),
                  pl.BlockSpec((seq, HEAD_A), lambda b, hh, g=g: (b, heads + g * HEADS_PER_GROUP + hh)),
                  pl.BlockSpec((seq, HEAD_A), lambda b, hh, g=g: (b, g * HEADS_PER_GROUP + hh))]
        args += [qk, qk, v]
    return pl.pallas_call(
        functools.partial(_prompt_attn_kernel, seq=seq),
        out_shape=jax.ShapeDtypeStruct((batch * seq, GROUP_W), BF16),
        grid=(batch, HEADS_PER_GROUP),
        in_specs=specs,
        out_specs=pl.BlockSpec((seq, HEAD_A), lambda b, hh: (b, hh)),
        scratch_shapes=[pltpu.VMEM((N_GROUPS, seq, HEAD_A), F32), pltpu.VMEM((N_GROUPS, seq, HEAD_A), F32)],
        compiler_params=_cparams(("parallel", "parallel")),
        name="prompt_attn",
    )(*args)


SAMPLE_PAIR = 2
SAMPLE_ATTN_VMEM = 56 * 1024 * 1024


def _sample_attn_kernel(q_ref, k_ref, v_ref, c0_ref, c1_ref, c2_ref, out_ref, *, t_new):
    heads = HEADS_PER_GROUP
    scale = HEAD_A ** -0.5
    nq = heads * t_new
    qi = lax.broadcasted_iota(jnp.int32, (nq, 1), 0)
    q_h, q_t = qi // t_new, qi % t_new
    per_class = WINDOW_KEYS * heads
    out_rows = []
    for e in range(SAMPLE_PAIR):
        rows = slice(e * t_new, (e + 1) * t_new)
        outs, lses = [], []
        for g in range(N_GROUPS):
            def heads_to_rows(ref):
                return jnp.concatenate([ref[rows, (g * heads + h) * HEAD_A:(g * heads + h + 1) * HEAD_A]
                                        for h in range(heads)], axis=0)

            q = heads_to_rows(q_ref).astype(BF16)
            if g == 0:
                k_c = c0_ref[e, :, 0].reshape(per_class, HEAD_A)
                v_c = c0_ref[e, :, 1].reshape(per_class, HEAD_A)
                ncache = per_class
            else:
                cref = c1_ref if g == 1 else c2_ref
                k_c = jnp.concatenate([cref[e, :, c, 0].reshape(per_class, HEAD_A) for c in range(t_new)], axis=0)
                v_c = jnp.concatenate([cref[e, :, c, 1].reshape(per_class, HEAD_A) for c in range(t_new)], axis=0)
                ncache = t_new * per_class
            k_all = jnp.concatenate([k_c, heads_to_rows(k_ref)], axis=0).astype(BF16)
            v_all = jnp.concatenate([v_c, heads_to_rows(v_ref)], axis=0).astype(BF16)
            s = lax.dot_general(q, k_all, (((1,), (1,)), ((), ())), preferred_element_type=F32) * scale
            kj = lax.broadcasted_iota(jnp.int32, (1, ncache + nq), 1)
            is_new = kj >= ncache
            nj = kj - ncache
            k_h = jnp.where(is_new, nj // t_new, kj % heads)
            if g == 0:
                ok = jnp.where(is_new, q_t - nj % t_new, kj // heads - q_t) >= 0
            else:
                ok = jnp.where(is_new, nj % t_new, kj // per_class) == q_t
            s = jnp.where(jnp.logical_and(ok, k_h == q_h), s, -jnp.inf)
            p, lse = _softmax_parts(s)
            outs.append(jnp.dot(p.astype(BF16), v_all, preferred_element_type=F32))
            lses.append(lse)
        merged = _merge_three(outs, lses)
        out_rows.append(jnp.concatenate([merged[h * t_new:(h + 1) * t_new] for h in range(heads)], axis=1))
    out_ref[...] = jnp.concatenate(out_rows, axis=0).astype(out_ref.dtype)


def _sample_attention(qk, v, caches, row0, dec_batch, t_new):
    n = qk.shape[0]
    nrow = SAMPLE_PAIR * t_new
    assert nrow == 8 and row0 % nrow == 0 and dec_batch % SAMPLE_PAIR == 0
    rb0 = row0 // nrow
    tail = caches[0].shape[2:]
    c1 = caches[1].reshape((dec_batch, WINDOW_KEYS, DILATIONS[1]) + tail)
    c2 = caches[2].reshape((dec_batch, WINDOW_KEYS, DILATIONS[2]) + tail)
    assert DILATIONS[1] == t_new
    kern = functools.partial(_sample_attn_kernel, t_new=t_new)
    class_blk = (SAMPLE_PAIR, WINDOW_KEYS, t_new) + tail
    return pl.pallas_call(
        kern,
        out_shape=jax.ShapeDtypeStruct((dec_batch * t_new, GROUP_W), BF16),
        grid=(dec_batch // SAMPLE_PAIR,),
        in_specs=[pl.BlockSpec((nrow, A_WIDTH), lambda i: (rb0 + i, 0)),
                  pl.BlockSpec((nrow, A_WIDTH), lambda i: (rb0 + i, 1)),
                  pl.BlockSpec((nrow, A_WIDTH), lambda i: (rb0 + i, 0)),
                  pl.BlockSpec((SAMPLE_PAIR, WINDOW_KEYS) + tail, lambda i: (i, 0, 0, 0, 0)),
                  pl.BlockSpec(class_blk, lambda i: (i, 0, 0, 0, 0, 0)),
                  pl.BlockSpec(class_blk, lambda i: (i, 0, 0, 0, 0, 0))],
        out_specs=pl.BlockSpec((nrow, GROUP_W), lambda i: (i, 0)),
        compiler_params=_cparams(("parallel",), vmem=SAMPLE_ATTN_VMEM),
        name="sample_attn",
    )(qk, qk, v, caches[0], c1, c2)


def _cache_roll_kernel(c_ref, n_ref, o_ref):
    last = pl.program_id(0) == pl.num_programs(0) - 1
    o_ref[...] = jnp.where(last, n_ref[...], c_ref[...])


def _roll_cache(cache, new, t_new):
    nb, length = cache.shape[0], cache.shape[1]
    assert length % t_new == 0
    nblk = length // t_new
    blk = (nb, t_new) + cache.shape[2:]
    return pl.pallas_call(
        _cache_roll_kernel,
        out_shape=jax.ShapeDtypeStruct(cache.shape, cache.dtype),
        grid=(nblk,),
        in_specs=[pl.BlockSpec(blk, lambda j: (0, jnp.minimum(j + 1, nblk - 1), 0, 0, 0)),
                  pl.BlockSpec(blk, lambda j: (0, 0, 0, 0, 0))],
        out_specs=pl.BlockSpec(blk, lambda j: (0, j, 0, 0, 0)),
        compiler_params=_cparams(("parallel",)),
        name=f"cache_roll_w{length}",
    )(cache, new)


def _roll_caches(caches, news, t_new):
    return [_roll_cache(c, x.astype(c.dtype), t_new) for c, x in zip(caches, news)]


def _split_bf16(x):
    hi = x.astype(BF16)
    lo = (x - hi.astype(F32)).astype(BF16)
    return hi, lo


def _dot3(a, b, dims):
    ah, al = _split_bf16(a)
    bh, bl = _split_bf16(b)
    dn = (dims, ((), ()))
    out = lax.dot_general(ah, bh, dn, preferred_element_type=F32)
    out = out + lax.dot_general(ah, bl, dn, preferred_element_type=F32)
    return out + lax.dot_general(al, bh, dn, preferred_element_type=F32)


def _cumsum_rows(tri, x):
    hi = x.astype(BF16)
    rem = x - hi.astype(F32)
    mid = rem.astype(BF16)
    lo = (rem - mid.astype(F32)).astype(BF16)
    return (jnp.dot(tri, hi, preferred_element_type=F32) + jnp.dot(tri, mid, preferred_element_type=F32)
            + jnp.dot(tri, lo, preferred_element_type=F32))


_NN = ((1,), (0,))
_NT = ((1,), (1,))
_TN = ((0,), (0,))


def _segsum_mat():
    r = lax.broadcasted_iota(jnp.int32, (LANES, LANES), 0) // HEAD_B
    c = lax.broadcasted_iota(jnp.int32, (LANES, LANES), 1) // HEAD_B
    return (r == c).astype(BF16)


def _segsum(x, ones_bd):
    hi, lo = _split_bf16(x)
    return (jnp.dot(hi, ones_bd, preferred_element_type=F32) + jnp.dot(lo, ones_bd, preferred_element_type=F32))


def _rwkv_prep_kernel(k_ref, wl_ref, al_ref, gl_ref, w2_ref, a2_ref, g2_ref, vec_ref,
                      lw_ref, km_ref, na_ref, nb_ref, g_ref):
    w0, a0, k_k, k_a = vec_ref[0:1, :], vec_ref[1:2, :], vec_ref[2:3, :], vec_ref[3:4, :]
    w_lin = w0 + jnp.dot(jnp.tanh(wl_ref[...]).astype(BF16), w2_ref[...].astype(BF16), preferred_element_type=F32)
    w_log = -jax.nn.softplus(-w_lin) - 0.5
    lw_ref[...] = -jnp.exp(w_log)
    a = jax.nn.sigmoid(a0 + jnp.dot(al_ref[...].astype(BF16), a2_ref[...].astype(BF16), preferred_element_type=F32))
    g_ref[...] = jnp.dot(jax.nn.sigmoid(gl_ref[...]).astype(BF16), g2_ref[...].astype(BF16), preferred_element_type=F32)
    k = k_ref[...]
    km_ref[...] = k * (1.0 + (a - 1.0) * k_a)
    kk = k * k_k
    ones_bd = _segsum_mat()
    for s in range(B_WIDTH // LANES):
        sl = slice(s * LANES, (s + 1) * LANES)
        kks = kk[:, sl]
        nrm = jnp.maximum(jnp.sqrt(_segsum(kks * kks, ones_bd)), 1e-12)
        kkn = kks / nrm
        na_ref[:, sl] = -kkn
        nb_ref[:, sl] = kkn * a[:, sl]


def _rwkv_prep(k, w_low, a_low, g_low, w2, a2, g2_pad, vecs, tm=256):
    n = k.shape[0]
    wide = pl.BlockSpec((tm, B_WIDTH), lambda i: (i, 0))
    full = lambda shape: pl.BlockSpec(shape, lambda i: (0, 0))
    sds = jax.ShapeDtypeStruct((n, B_WIDTH), F32)
    return pl.pallas_call(
        _rwkv_prep_kernel,
        out_shape=(sds,) * 5,
        grid=(n // tm,),
        in_specs=[wide, pl.BlockSpec((tm, DECAY_LORA), lambda i: (i, 0)), pl.BlockSpec((tm, AAA_LORA), lambda i: (i, 0)),
                  pl.BlockSpec((tm, GATE_LORA_PAD), lambda i: (i, 0)),
                  full((DECAY_LORA, B_WIDTH)), full((AAA_LORA, B_WIDTH)), full((GATE_LORA_PAD, B_WIDTH)),
                  full((4, B_WIDTH))],
        out_specs=(wide,) * 5,
        compiler_params=_cparams(("parallel",)),
        name="rwkv_prep",
    )(k, w_low, a_low, g_low, w2, a2, g2_pad, vecs)


def _each(fn, *lists):
    return [fn(*args) for args in zip(*lists)]


def _dot1(a, b, dims):
    return lax.dot_general(a.astype(BF16), b.astype(BF16), (dims, ((), ())), preferred_element_type=F32)


def _mm_each(xs, ys, dims):
    return _each(lambda x, y: _dot1(x, y, dims), xs, ys)


def _tri_inverse(a_bds, c):
    n = 2 * c
    r = lax.broadcasted_iota(jnp.int32, (n, n), 0)
    q = lax.broadcasted_iota(jnp.int32, (n, n), 1)
    eye = (r == q).astype(F32)
    base = 8
    in_base = r // base == q // base
    a0 = [jnp.where(in_base, a, 0.0) for a in a_bds]
    a2 = _mm_each(a0, a0, _NN)
    a4 = _mm_each(a2, a2, _NN)
    x = [eye + a for a in a0]
    x = _each(jnp.add, x, _mm_each(x, a2, _NN))
    x = _each(jnp.add, x, _mm_each(x, a4, _NN))
    size = base
    while size < c:
        below = jnp.logical_and(r // (2 * size) == q // (2 * size), r // size == q // size + 1)
        off = [jnp.where(below, a, 0.0) for a in a_bds]
        x = _each(jnp.add, x, _mm_each(_mm_each(x, off, _NN), x, _NN))
        size *= 2
    return x


def _rwkv_chunk(chains, c):
    n = 2 * c
    lane = lax.broadcasted_iota(jnp.int32, (c, LANES), 1)
    m0 = (lane < HEAD_B).astype(F32)
    m1 = 1.0 - m0
    stack = lambda x: jnp.concatenate([x * m0, x * m1], axis=0)
    r, lw, k, v, a, b, s_bd = (list(col) for col in zip(*chains))

    tr = lax.broadcasted_iota(jnp.int32, (c, c), 0)
    tc = lax.broadcasted_iota(jnp.int32, (c, c), 1)
    tri = (tr >= tc).astype(BF16)
    cl = [_cumsum_rows(tri, x) for x in lw]
    cl_end = [x[c - 1:c, :] for x in cl]
    r_t = _each(lambda x, d: stack(x * jnp.exp(d)), r, cl)
    a_t = _each(lambda x, d, w: stack(x * jnp.exp(d - w)), a, cl, lw)
    b_t = _each(lambda x, d: stack(x * jnp.exp(-d)), b, cl)
    k_t = _each(lambda x, d: stack(x * jnp.exp(-d)), k, cl)
    bk_e = _each(lambda x, y, d, e: jnp.concatenate([stack(x * jnp.exp(e - d)), stack(y * jnp.exp(e - d))], axis=0),
                 b, k, cl, cl_end)
    v_s = [stack(x) for x in v]

    rr = lax.broadcasted_iota(jnp.int32, (n, n), 0)
    qq = lax.broadcasted_iota(jnp.int32, (n, n), 1)
    same = rr // c == qq // c
    strict = jnp.logical_and(same, rr > qq)
    incl = jnp.logical_and(same, rr >= qq)
    ar = _each(lambda x, y: jnp.concatenate([x, y], axis=0), a_t, r_t)
    gb = _mm_each(ar, b_t, _NT)
    gk = _mm_each(ar, k_t, _NT)
    ab = [jnp.where(strict, g[:n], 0.0) for g in gb]
    rb = [jnp.where(incl, g[n:], 0.0) for g in gb]
    ak = [jnp.where(strict, g[:n], 0.0) for g in gk]
    rk = [jnp.where(incl, g[n:], 0.0) for g in gk]

    us = _mm_each(ar, s_bd, _NT)
    akv = _mm_each(ak, v_s, _NN)
    rkv = _mm_each(rk, v_s, _NN)
    t_inv = _tri_inverse(ab, c)
    u = _mm_each(t_inv, _each(lambda x, y: x[:n] + y, us, akv), _NN)
    rbu = _mm_each(rb, u, _NN)
    y_s = _each(lambda x, y, z: x[n:] + y + z, us, rbu, rkv)
    upd = _mm_each(_each(lambda x, y: jnp.concatenate([x, y], axis=0), u, v_s), bk_e, _TN)
    s_new = _each(lambda s, e, d: s * jnp.exp(e) + d, s_bd, cl_end, upd)
    return [(y[:c] + y[c:], s) for y, s in zip(y_s, s_new)]


RWKV_PAIRS = 16


def _rwkv_prompt_kernel(r_ref, lw_ref, k_ref, v_ref, a_ref, b_ref, y_ref, s_ref, s_acc):
    ci = pl.program_id(2)

    @pl.when(ci == 0)
    def _():
        s_acc[...] = jnp.zeros_like(s_acc)

    lanes = [slice(p * LANES, (p + 1) * LANES) for p in range(RWKV_PAIRS)]
    chains = [(r_ref[:, sl], lw_ref[:, sl], k_ref[:, sl], v_ref[:, sl], a_ref[:, sl], b_ref[:, sl], s_acc[p])
              for p, sl in enumerate(lanes)]
    res = _rwkv_chunk(chains, CHUNK)
    for p, sl in enumerate(lanes):
        y_ref[:, sl] = res[p][0]
        s_acc[p] = res[p][1]

    @pl.when(ci == pl.num_programs(2) - 1)
    def _():
        for p in range(RWKV_PAIRS):
            _store_pair_state(s_ref, p, s_acc[p])


def _pair_state(s_ref, p):
    zero = jnp.zeros((HEAD_B, HEAD_B), F32)
    top = jnp.concatenate([s_ref[2 * p], zero], axis=1)
    bot = jnp.concatenate([zero, s_ref[2 * p + 1]], axis=1)
    return jnp.concatenate([top, bot], axis=0)


def _store_pair_state(s_ref, p, s_bd):
    s_ref[2 * p] = s_bd[:HEAD_B, :HEAD_B]
    s_ref[2 * p + 1] = s_bd[HEAD_B:, HEAD_B:]


def _rwkv_prompt(r, lw, k, v, a, b, batch, seq):
    npair = B_WIDTH // LANES
    nchunk = seq // CHUNK
    wide = RWKV_PAIRS * LANES
    spec = pl.BlockSpec((CHUNK, wide), lambda bi, p, ci: (bi * nchunk + ci, p))
    return pl.pallas_call(
        _rwkv_prompt_kernel,
        out_shape=(jax.ShapeDtypeStruct((batch * seq, B_WIDTH), F32),
                   jax.ShapeDtypeStruct((batch, 2 * npair, HEAD_B, HEAD_B), F32)),
        grid=(batch, npair // RWKV_PAIRS, nchunk),
        in_specs=[spec] * 6,
        out_specs=(spec, pl.BlockSpec((None, 2 * RWKV_PAIRS, HEAD_B, HEAD_B), lambda bi, p, ci: (bi, p, 0, 0))),
        scratch_shapes=[pltpu.VMEM((RWKV_PAIRS, LANES, LANES), F32)],
        compiler_params=_cparams(("parallel", "parallel", "arbitrary")),
        name="rwkv_prompt",
    )(r, lw, k, v, a, b)


def _rwkv_sample_kernel(r_ref, lw_ref, k_ref, v_ref, a_ref, b_ref, s0_ref, y_ref, s_ref, *, t_new):
    c = SAMPLE_CHUNK
    row = lax.broadcasted_iota(jnp.int32, (c, LANES), 0)
    def sel(ref, sl, e):
        x = ref[:, sl]
        if e:
            x = pltpu.roll(x, c - e * t_new, 0)
        return jnp.where(row < t_new, x, 0.0)

    ids = [(p, e) for p in range(RWKV_PAIRS) for e in range(SAMPLE_PAIR)]
    chains = []
    for p, e in ids:
        sl = slice(p * LANES, (p + 1) * LANES)
        chains.append(tuple(sel(ref, sl, e) for ref in (r_ref, lw_ref, k_ref, v_ref, a_ref, b_ref))
                      + (_pair_state(s0_ref.at[e], p),))
    res = dict(zip(ids, _rwkv_chunk(chains, c)))
    for p in range(RWKV_PAIRS):
        y_all = res[(p, 0)][0]
        for e in range(1, SAMPLE_PAIR):
            y_all = jnp.where(row // t_new == e, pltpu.roll(res[(p, e)][0], e * t_new, 0), y_all)
        y_ref[:, p * LANES:(p + 1) * LANES] = y_all
        for e in range(SAMPLE_PAIR):
            _store_pair_state(s_ref.at[e], p, res[(p, e)][1])


def _rwkv_sample(r, lw, k, v, a, b, s0, row0, dec_batch, t_new):
    npair = B_WIDTH // LANES
    nrow = SAMPLE_PAIR * t_new
    rb0 = row0 // nrow
    wide = RWKV_PAIRS * LANES
    spec = pl.BlockSpec((nrow, wide), lambda i, p: (rb0 + i, p))
    s_spec = pl.BlockSpec((SAMPLE_PAIR, 2 * RWKV_PAIRS, HEAD_B, HEAD_B), lambda i, p: (i, p, 0, 0))
    return pl.pallas_call(
        functools.partial(_rwkv_sample_kernel, t_new=t_new),
        out_shape=(jax.ShapeDtypeStruct((dec_batch * t_new, B_WIDTH), F32),
                   jax.ShapeDtypeStruct(s0.shape, F32)),
        grid=(dec_batch // SAMPLE_PAIR, npair // RWKV_PAIRS),
        in_specs=[spec] * 6 + [s_spec],
        out_specs=(pl.BlockSpec((nrow, wide), lambda i, p: (i, p)), s_spec),
        compiler_params=_cparams(("parallel", "parallel")),
        name="rwkv_sample",
    )(r, lw, k, v, a, b, s0)


def _rwkv_post_kernel(y_ref, r_ref, k_ref, v_ref, g_ref, vec_ref, o_ref):
    ones_bd = _segsum_mat()
    inv = 1.0 / HEAD_B
    for s in range(B_WIDTH // LANES):
        sl = slice(s * LANES, (s + 1) * LANES)
        y = y_ref[:, sl]
        mean = _segsum(y, ones_bd) * inv
        dlt = y - mean
        var = _segsum(dlt * dlt, ones_bd) * inv
        yn = dlt * lax.rsqrt(var + GN_EPS) * vec_ref[0:1, sl] + vec_ref[1:2, sl]
        bonus = _segsum(r_ref[:, sl] * k_ref[:, sl] * vec_ref[2:3, sl], ones_bd) * v_ref[:, sl]
        o_ref[:, sl] = ((yn + bonus) * g_ref[:, sl]).astype(o_ref.dtype)


def _rwkv_post(y, r, k_mod, v, g, vecs, tm=256):
    n = y.shape[0]
    wide = pl.BlockSpec((tm, B_WIDTH), lambda i: (i, 0))
    return pl.pallas_call(
        _rwkv_post_kernel,
        out_shape=jax.ShapeDtypeStruct((n, B_WIDTH), BF16),
        grid=(n // tm,),
        in_specs=[wide] * 5 + [pl.BlockSpec((3, B_WIDTH), lambda i: (0, 0))],
        out_specs=wide,
        compiler_params=_cparams(("parallel",)),
        name="rwkv_post",
    )(y, r, k_mod, v, g, vecs)


def _read_kernel(ao_ref, bo_ref, wa_ref, wb_ref, ga_ref, gb_ref, o_ref):
    ra = jnp.dot(ao_ref[...], wa_ref[...].astype(BF16), preferred_element_type=F32)
    rb = jnp.dot(bo_ref[...], wb_ref[...].astype(BF16), preferred_element_type=F32)
    o_ref[...] = (ga_ref[...] * ra + gb_ref[...] * rb).astype(o_ref.dtype)


def _gated_read(a_out, b_out, w_read_a, w_read_b, gates, tm, tn=512):
    n = a_out.shape[0]
    ncol = D_MODEL // tn
    return pl.pallas_call(
        _read_kernel,
        out_shape=jax.ShapeDtypeStruct((n, D_MODEL), BF16),
        grid=(ncol, n // tm),
        in_specs=[pl.BlockSpec((tm, GROUP_W), lambda j, i: (i, 0)),
                  pl.BlockSpec((tm, B_WIDTH), lambda j, i: (i, 0)),
                  pl.BlockSpec((GROUP_W, tn), lambda j, i: (0, j)),
                  pl.BlockSpec((B_WIDTH, tn), lambda j, i: (0, j)),
                  pl.BlockSpec((tm, tn), lambda j, i: (i, j)),
                  pl.BlockSpec((tm, tn), lambda j, i: (i, j + ncol))],
        out_specs=pl.BlockSpec((tm, tn), lambda j, i: (i, j)),
        compiler_params=_cparams(("parallel", "parallel")),
        name="gated_read",
    )(a_out, b_out, w_read_a, w_read_b, gates, gates)


def _residual_epilogue(acc, x_ref):
    return x_ref[...] + acc


def _pack_halves(h):
    half = h.shape[1] // 2
    top = pltpu.bitcast(h[:, :half].astype(BF16).astype(F32), jnp.uint32)
    bot = pltpu.bitcast(h[:, half:].astype(BF16).astype(F32), jnp.uint32)
    return top | (bot >> 16)


def _unpack_halves(p):
    top = pltpu.bitcast(p & jnp.uint32(0xFFFF0000), F32).astype(BF16)
    bot = pltpu.bitcast(p << 16, F32).astype(BF16)
    return top, bot


def _ffn_norm_router_kernel(u_ref, g_ref, wr_ref, hb_ref, lg_ref):
    x = u_ref[...]
    ms = jnp.mean(x * x, axis=-1, keepdims=True)
    h = x * lax.rsqrt(ms + NORM_EPS) * g_ref[...]
    hb_ref[...] = _pack_halves(h)
    lg_ref[...] = _dot3(h, wr_ref[...], _NN)


def _ffn_norm_router(u, gain, w_router, tm=256):
    n, d = u.shape
    return pl.pallas_call(
        _ffn_norm_router_kernel,
        out_shape=(jax.ShapeDtypeStruct((n, d // 2), jnp.uint32), jax.ShapeDtypeStruct((n, ROUTER_PAD), F32)),
        grid=(n // tm,),
        in_specs=[pl.BlockSpec((tm, d), lambda i: (i, 0)), pl.BlockSpec((1, d), lambda i: (0, 0)),
                  pl.BlockSpec((d, ROUTER_PAD), lambda i: (0, 0))],
        out_specs=(pl.BlockSpec((tm, d // 2), lambda i: (i, 0)), pl.BlockSpec((tm, ROUTER_PAD), lambda i: (i, 0))),
        compiler_params=_cparams(("parallel",)),
        name="ffn_norm_router",
    )(u, gain.reshape(1, d), w_router)


def _route(logits, group_b, expert_b):
    n = logits.shape[0]
    group_logits = logits[:, :N_EXPERT_GROUPS] + group_b.astype(F32)
    group = jnp.argmax(group_logits, axis=-1).astype(jnp.int32)
    p_group = jnp.take_along_axis(jax.nn.softmax(group_logits, axis=-1), group[:, None], axis=-1)
    expert_logits = (logits[:, N_EXPERT_GROUPS:N_EXPERT_GROUPS + N_EXPERTS] + expert_b.astype(F32)).reshape(
        n, N_EXPERT_GROUPS, EXPERTS_PER_GROUP)
    in_group = jnp.take_along_axis(expert_logits, group[:, None, None], axis=1)[:, 0]
    top_val, top_idx = lax.top_k(in_group, TOP_K)
    gate = p_group * jax.nn.softmax(top_val, axis=-1)
    expert_id = group[:, None] * EXPERTS_PER_GROUP + top_idx.astype(jnp.int32)
    return expert_id, gate


FFN_CHUNKS = 1
FFN_VMEM = 56 * 1024 * 1024


def _moe_plan(expert_id):
    n_assign = expert_id.size
    n_blocks = n_assign // ROW_BLOCK + N_EXPERTS
    n_rows = n_blocks * ROW_BLOCK
    max_items = FFN_CHUNKS * n_blocks
    i32 = jnp.int32
    e_flat = expert_id.reshape(n_assign)
    onehot = (e_flat[:, None] == jnp.arange(N_EXPERTS, dtype=i32)[None, :]).astype(i32)
    running = jnp.cumsum(onehot, axis=0)
    rank = jnp.sum(onehot * (running - 1), axis=1)
    counts = running[-1]
    nblk = (counts + ROW_BLOCK - 1) // ROW_BLOCK
    blk_start = jnp.cumsum(nblk) - nblk
    pos = (blk_start[e_flat] * ROW_BLOCK + rank).astype(i32)
    row_src = jnp.zeros((n_rows,), i32).at[pos].set(jnp.arange(n_assign, dtype=i32) // TOP_K)
    item_cnt = FFN_CHUNKS * nblk
    item_end = jnp.cumsum(item_cnt)
    item_start = item_end - item_cnt
    total = item_end[-1]
    idx = jnp.arange(max_items, dtype=i32)
    ic = jnp.minimum(idx, total - 1)
    it_e = jnp.minimum(jnp.sum((item_end[None, :] <= ic[:, None]).astype(i32), axis=1), N_EXPERTS - 1)
    within = ic - item_start[it_e]
    nb = jnp.maximum(nblk[it_e], 1)
    it_wc = within // nb
    valid = idx < total
    first = jnp.logical_and(valid, within % nb == 0)
    tail = idx - total
    it_blk = jnp.where(valid, blk_start[it_e] + within % nb, total // FFN_CHUNKS + tail // FFN_CHUNKS)
    it_c = jnp.where(valid, it_wc, tail % FFN_CHUNKS)
    it_slot = jnp.maximum(jnp.cumsum(first.astype(i32)) - 1, 0) % 2
    owner = jnp.where(nblk > 0, jnp.arange(N_EXPERTS, dtype=i32), N_EXPERTS)
    later = jnp.concatenate([lax.cummin(owner[::-1])[::-1][1:], jnp.full((1,), N_EXPERTS, i32)])
    it_next = jnp.where(later < N_EXPERTS, later, -1)[it_e]
    return dict(pos=pos, row_src=row_src, it_e=it_e, it_wc=it_wc.astype(i32), it_c=it_c.astype(i32),
                it_blk=it_blk.astype(i32), it_valid=valid.astype(i32), it_first=first.astype(i32),
                it_slot=it_slot.astype(i32), it_next=it_next.astype(i32), n_rows=n_rows, max_items=max_items)


GATHER_ROWS = 128


def _dispatch_kernel(src_ref, x_hbm, o_ref, buf, sem):
    i = pl.program_id(0)
    slot = i % 2

    def row_copy(slot_, j, src_row):
        return pltpu.make_async_copy(x_hbm.at[pl.ds(src_row, 1)], buf.at[slot_, pl.ds(j, 1)], sem.at[slot_])

    def issue(block, slot_):
        base = block * GATHER_ROWS

        def body(j, carry):
            row_copy(slot_, j, src_ref[base + j]).start()
            return carry

        lax.fori_loop(0, GATHER_ROWS, body, 0, unroll=8)

    @pl.when(i == 0)
    def _():
        issue(0, 0)

    @pl.when(i + 1 < pl.num_programs(0))
    def _():
        issue(i + 1, 1 - slot)

    for j in range(GATHER_ROWS):
        row_copy(slot, j, 0).wait()
    o_ref[...] = buf[slot]


def _dispatch_rows(x, row_src, n_rows):
    d = x.shape[1]
    return pl.pallas_call(
        _dispatch_kernel,
        out_shape=jax.ShapeDtypeStruct((n_rows, d), x.dtype),
        grid_spec=pltpu.PrefetchScalarGridSpec(
            num_scalar_prefetch=1,
            grid=(n_rows // GATHER_ROWS,),
            in_specs=[pl.BlockSpec(memory_space=pl.ANY)],
            out_specs=pl.BlockSpec((GATHER_ROWS, d), lambda i, src: (i, 0)),
            scratch_shapes=[pltpu.VMEM((2, GATHER_ROWS, d), x.dtype), pltpu.SemaphoreType.DMA((2,))],
        ),
        compiler_params=_cparams(("arbitrary",)),
        name="moe_dispatch",
    )(row_src, x)


def _ffn_kernel(e_ref, blk_ref, valid_ref, first_ref, slot_ref, next_ref, *refs, mode):
    if mode == "up":
        x_ref, g_ref, w_hbm, o_ref, w_stage, w_bf, sem = refs
    else:
        x_ref, w_hbm, o_ref, w_stage, w_bf, sem = refs
    i = pl.program_id(0)

    def weight_copy(expert, slot):
        return pltpu.make_async_copy(w_hbm.at[expert], w_stage.at[slot], sem.at[slot])

    @pl.when(i == 0)
    def _():
        weight_copy(e_ref[0], 0).start()

    @pl.when(first_ref[i] == 1)
    def _():
        slot = slot_ref[i]
        weight_copy(e_ref[i], slot).wait()
        w_bf[...] = w_stage[slot].astype(BF16)

        @pl.when(next_ref[i] >= 0)
        def _():
            weight_copy(next_ref[i], 1 - slot).start()

    @pl.when(valid_ref[i] == 0)
    def _():
        o_ref[...] = jnp.zeros_like(o_ref)

    @pl.when(valid_ref[i] == 1)
    def _():
        if mode == "down":
            acc = jnp.dot(x_ref[...], w_bf[...], preferred_element_type=F32)
        else:
            top, bot = _unpack_halves(x_ref[...])
            half = top.shape[1]
            acc = (jnp.dot(top, w_bf[:half, :], preferred_element_type=F32)
                   + jnp.dot(bot, w_bf[half:, :], preferred_element_type=F32))
        if mode == "up":
            acc = jax.nn.silu(g_ref[...]) * acc
        o_ref[...] = acc.astype(o_ref.dtype)


def _expert_matmul(mode, rows, w, plan, extra=None):
    n_rows, row_w = rows.shape
    k, out_w = w.shape[1], w.shape[2]
    assert FFN_CHUNKS == 1
    prefetch = (plan["it_e"], plan["it_blk"], plan["it_valid"], plan["it_first"], plan["it_slot"], plan["it_next"])
    row_spec = lambda width: pl.BlockSpec((ROW_BLOCK, width), lambda i, e, b, v, f, s, nx: (b[i], 0))
    in_specs = [row_spec(row_w)] + ([row_spec(out_w)] if mode == "up" else [])
    in_specs.append(pl.BlockSpec(memory_space=pl.ANY))
    args = (rows,) + ((extra,) if mode == "up" else ()) + (w,)
    return pl.pallas_call(
        functools.partial(_ffn_kernel, mode=mode),
        out_shape=jax.ShapeDtypeStruct((n_rows, out_w), BF16 if mode == "up" else F32),
        grid_spec=pltpu.PrefetchScalarGridSpec(
            num_scalar_prefetch=len(prefetch),
            grid=(plan["max_items"],),
            in_specs=in_specs,
            out_specs=row_spec(out_w),
            scratch_shapes=[pltpu.VMEM((2, k, out_w), F32), pltpu.VMEM((k, out_w), BF16),
                            pltpu.SemaphoreType.DMA((2,))],
        ),
        compiler_params=_cparams(("arbitrary",), vmem=FFN_VMEM),
        name=f"expert_ffn_{mode}",
    )(*prefetch, *args)


def _expert_ffn(x_sorted, plan, w_gate, w_up, w_down):
    gate = _expert_matmul("gate", x_sorted, w_gate, plan)
    hid = _expert_matmul("up", x_sorted, w_up, plan, extra=gate)
    return _expert_matmul("down", hid, w_down, plan)


COMBINE_ROWS = 64


def _combine_kernel(pos_ref, u_ref, g_ref, y_hbm, o_ref, buf, sem):
    i = pl.program_id(0)
    slot = i % 2

    def row_copy(slot_, s, j, src_row):
        return pltpu.make_async_copy(y_hbm.at[pl.ds(src_row, 1)], buf.at[slot_, s, pl.ds(j, 1)], sem.at[slot_])

    def issue(tile, slot_):
        base = tile * COMBINE_ROWS

        def body(j, carry):
            for s in range(TOP_K):
                row_copy(slot_, s, j, pos_ref[(base + j) * TOP_K + s]).start()
            return carry

        lax.fori_loop(0, COMBINE_ROWS, body, 0, unroll=8)

    @pl.when(i == 0)
    def _():
        issue(0, 0)

    @pl.when(i + 1 < pl.num_programs(0))
    def _():
        issue(i + 1, 1 - slot)

    for j in range(COMBINE_ROWS):
        for s in range(TOP_K):
            row_copy(slot, s, j, 0).wait()
    g = g_ref[...]
    moe = buf[slot, 0] * g[:, 0:1] + buf[slot, 1] * g[:, 1:2]
    o_ref[...] = u_ref[...] + moe


def _combine(u, gate, y_sorted, pos):
    n, d = u.shape
    return pl.pallas_call(
        _combine_kernel,
        out_shape=jax.ShapeDtypeStruct((n, d), F32),
        grid_spec=pltpu.PrefetchScalarGridSpec(
            num_scalar_prefetch=1,
            grid=(n // COMBINE_ROWS,),
            in_specs=[pl.BlockSpec((COMBINE_ROWS, d), lambda i, p: (i, 0)),
                      pl.BlockSpec((COMBINE_ROWS, TOP_K), lambda i, p: (i, 0)),
                      pl.BlockSpec(memory_space=pl.ANY)],
            out_specs=pl.BlockSpec((COMBINE_ROWS, d), lambda i, p: (i, 0)),
            scratch_shapes=[pltpu.VMEM((2, TOP_K, COMBINE_ROWS, d), F32), pltpu.SemaphoreType.DMA((2,))],
        ),
        compiler_params=_cparams(("arbitrary",)),
        name="moe_combine",
    )(pos, u, gate, y_sorted)


def _pick_tm(n):
    for tm in (1088, 1024, 544, 512, 272, 256, 128):
        if n % tm == 0:
            return tm
    raise ValueError(f"row count {n} is not a multiple of 128")


def kernel(x_prompt, x_sample, cache_kv_w128, cache_kv_w512, cache_kv_w2048, state_shift, state_wkv, norm_mix, w_in, q_norm, k_norm, mu_shift, rwkv_w0, rwkv_w2, rwkv_a0, rwkv_a2, rwkv_g2, rwkv_k_k, rwkv_k_a, rwkv_r_k, rwkv_ln_w, rwkv_ln_b, w_read_a, w_read_b, w_o, norm_ffn, router_group_w, router_group_b, router_expert_w, router_expert_b, expert_w_gate, expert_w_up, expert_w_down):
    batch, seq, d = x_prompt.shape
    dec_batch, t_new, _ = x_sample.shape
    n_p, n_s = batch * seq, dec_batch * t_new
    n = n_p + n_s
    past = cache_kv_w2048.shape[1]
    assert d == D_MODEL and seq % (DILATIONS[2] * ROW_BLOCK) == 0 and seq % CHUNK == 0
    assert cache_kv_w128.shape[1] == 128 and cache_kv_w512.shape[1] == 512 and past == 2048
    tm = _pick_tm(n)

    x = jnp.concatenate([x_prompt.reshape(n_p, d), x_sample.reshape(n_s, d)], axis=0)
    h = _rmsnorm(x, norm_mix)
    shift_p = h[seq - 1:n_p:seq]
    shift_s = h[n_p + t_new - 1::t_new]
    first_rows = jnp.broadcast_to(state_shift[:, None].astype(F32), (dec_batch, t_new, d)).reshape(n_s, d)
    hb, mixes = _token_shift_mix(h, first_rows, mu_shift, n_p, seq, t_new)

    positions = jnp.concatenate([jnp.tile(jnp.arange(seq, dtype=jnp.int32), batch),
                                 jnp.tile(past + jnp.arange(t_new, dtype=jnp.int32), dec_batch)])
    rope_c, rope_a, rope_b = _rope_tables(positions)
    gains = jnp.stack([q_norm, k_norm]).reshape(2, 1, HEAD_A).astype(F32)
    tn = GROUP_W
    row_spec = pl.BlockSpec((tm, HEAD_A), lambda j, i: (i, 0))
    w_in_t = w_in.T
    proj = functools.partial(_matmul, w=w_in_t, w_transposed=True, tm=tm)
    qk = proj(hb, col_off=0, width=2 * A_WIDTH, tn=tn, epilogue=_qk_epilogue,
              extra=(gains, rope_c, rope_a, rope_b),
              extra_specs=(pl.BlockSpec((None, 1, HEAD_A), lambda j, i: (j // N_GROUPS, 0, 0)),
                           row_spec, row_spec, row_spec), name="proj_qk")
    v_a = proj(hb, col_off=2 * A_WIDTH, width=A_WIDTH, tn=tn, name="proj_v")
    gates = proj(hb, col_off=3 * A_WIDTH, width=2 * D_MODEL, tn=tn, epilogue=_sigmoid_epilogue, name="proj_gates")
    r_b = proj(mixes, a_sel=0, col_off=X_COLS, width=B_WIDTH, tn=tn, name="proj_r")
    k_b = proj(mixes, a_sel=1, col_off=X_COLS + B_WIDTH, width=B_WIDTH, tn=tn, name="proj_k")
    v_b = proj(mixes, a_sel=2, col_off=X_COLS + 2 * B_WIDTH, width=B_WIDTH, tn=tn, name="proj_vb")
    lora0 = X_COLS + 3 * B_WIDTH
    w_low = proj(mixes, a_sel=3, col_off=lora0, width=DECAY_LORA, tn=DECAY_LORA, name="proj_wlow")
    a_low = proj(mixes, a_sel=4, col_off=lora0 + DECAY_LORA, width=AAA_LORA, tn=AAA_LORA, name="proj_alow")
    g_low = proj(mixes, a_sel=5, col_off=lora0 + DECAY_LORA + AAA_LORA, width=GATE_LORA_PAD, tn=DECAY_LORA,
                 epilogue=functools.partial(_valid_cols_epilogue, valid=GATE_LORA), name="proj_glow")

    a_out_p = _prompt_attention(qk, v_a, batch, seq)
    caches = (cache_kv_w128, cache_kv_w512, cache_kv_w2048)
    a_out_s = _sample_attention(qk, v_a, caches, n_p, dec_batch, t_new)
    a_out = jnp.concatenate([a_out_p, a_out_s], axis=0)

    def kv_rows(g, rows):
        kk = qk[rows, A_WIDTH + g * GROUP_W:A_WIDTH + (g + 1) * GROUP_W]
        vv = v_a[rows, g * GROUP_W:(g + 1) * GROUP_W]
        return jnp.stack([kk.reshape(-1, HEADS_PER_GROUP, HEAD_A), vv.reshape(-1, HEADS_PER_GROUP, HEAD_A)], axis=1)

    kv_prompt = []
    for g in range(N_GROUPS):
        keep = min(caches[g].shape[1], seq)
        full = kv_rows(g, slice(0, n_p)).reshape(batch, seq, 2, HEADS_PER_GROUP, HEAD_A)
        kv_prompt.append(full[:, seq - keep:])
    news = [kv_rows(g, slice(n_p, n)).reshape(dec_batch, t_new, 2, HEADS_PER_GROUP, HEAD_A) for g in range(N_GROUPS)]
    kv_sample = _roll_caches(caches, news, t_new)

    g2_pad = jnp.pad(rwkv_g2, ((0, GATE_LORA_PAD - GATE_LORA), (0, 0)))
    vec_prep = jnp.stack([rwkv_w0, rwkv_a0, rwkv_k_k, rwkv_k_a]).astype(F32)
    lw, k_mod, neg_kk, kk_a, g_out = _rwkv_prep(k_b, w_low, a_low, g_low, rwkv_w2, rwkv_a2, g2_pad, vec_prep)
    y_p, s_p = _rwkv_prompt(r_b, lw, k_mod, v_b, neg_kk, kk_a, batch, seq)
    y_s, s_s = _rwkv_sample(r_b, lw, k_mod, v_b, neg_kk, kk_a, state_wkv.astype(F32), n_p, dec_batch, t_new)
    y_b = jnp.concatenate([y_p, y_s], axis=0)
    vec_post = jnp.stack([rwkv_ln_w, rwkv_ln_b, rwkv_r_k.reshape(B_WIDTH)]).astype(F32)
    b_out = _rwkv_post(y_b, r_b, k_mod, v_b, g_out, vec_post)

    merged = _gated_read(a_out, b_out, w_read_a, w_read_b, gates, tm)
    u = _matmul(merged, w_o, tm=tm, tn=tn, epilogue=_residual_epilogue, extra=(x,),
                extra_specs=(pl.BlockSpec((tm, tn), lambda j, i: (i, j)),), name="proj_out")
    w_router = jnp.pad(jnp.concatenate([router_group_w, router_expert_w], axis=1),
                       ((0, 0), (0, ROUTER_PAD - N_EXPERT_GROUPS - N_EXPERTS)))
    hn, logits = _ffn_norm_router(u, norm_ffn, w_router)
    expert_id, gate = _route(logits, router_group_b, router_expert_b)
    plan = _moe_plan(expert_id)
    x_sorted = _dispatch_rows(hn, plan["row_src"], plan["n_rows"])
    y_sorted = _expert_ffn(x_sorted, plan, expert_w_gate, expert_w_up, expert_w_down)
    y = _combine(u, gate, y_sorted, plan["pos"])

    return (y[:n_p].reshape(batch, seq, d), y[n_p:].reshape(dec_batch, t_new, d),
            kv_prompt[0], kv_prompt[1], kv_prompt[2], shift_p, s_p.astype(state_wkv.dtype),
            kv_sample[0], kv_sample[1], kv_sample[2], shift_s, s_s.astype(state_wkv.dtype))
```

```python
import functools

import jax
import jax.numpy as jnp
from jax import lax
from jax.experimental import pallas as pl
from jax.experimental.pallas import tpu as pltpu

F32 = jnp.float32
BF16 = jnp.bfloat16

LANES = 128
VMEM_LIMIT = 48 * 1024 * 1024

D_MODEL = 4096
HEAD_A = 128
N_GROUPS = 3
HEADS_PER_GROUP = 4
GROUP_W = HEADS_PER_GROUP * HEAD_A
A_WIDTH = N_GROUPS * GROUP_W
DILATIONS = (1, 4, 16)
WINDOW_KEYS = 128
ROPE_THETA = 500000.0
ROPE_DIM = HEAD_A // 4
ROPE_HALF = ROPE_DIM // 2
HEAD_B = 64
B_WIDTH = D_MODEL // 2
DECAY_LORA = 128
AAA_LORA = 128
GATE_LORA = 480
GATE_LORA_PAD = 512
GN_EPS = 64e-5
NORM_EPS = 1e-6
X_COLS = 3 * A_WIDTH + 2 * D_MODEL
N_EXPERT_GROUPS = 8
EXPERTS_PER_GROUP = 8
N_EXPERTS = 64
TOP_K = 2
D_EXPERT = 1024
ROW_BLOCK = 128
ROUTER_PAD = 128
CHUNK = 64
SAMPLE_CHUNK = 8


def _cparams(sem, vmem=VMEM_LIMIT):
    return pltpu.CompilerParams(dimension_semantics=sem, vmem_limit_bytes=vmem)


def _rmsnorm_kernel(x_ref, g_ref, o_ref):
    x = x_ref[...]
    ms = jnp.mean(x * x, axis=-1, keepdims=True)
    o_ref[...] = x * lax.rsqrt(ms + NORM_EPS) * g_ref[...]


def _rmsnorm(x, gain, tm=256):
    n, d = x.shape
    return pl.pallas_call(
        _rmsnorm_kernel,
        out_shape=jax.ShapeDtypeStruct((n, d), F32),
        grid=(n // tm,),
        in_specs=[pl.BlockSpec((tm, d), lambda i: (i, 0)), pl.BlockSpec((1, d), lambda i: (0, 0))],
        out_specs=pl.BlockSpec((tm, d), lambda i: (i, 0)),
        compiler_params=_cparams(("parallel",)),
        name="rmsnorm",
    )(x, gain.reshape(1, d))


SUBLANES = 8


def _mix_kernel(h_ref, before_ref, first_ref, mu_ref, hb_ref, m_ref, *, n_prompt, seq, t_new):
    i = pl.program_id(0)
    tm = h_ref.shape[0]
    h = h_ref[...]
    row = lax.broadcasted_iota(jnp.int32, (tm, 1), 0)
    prev = jnp.where(row == 0, before_ref[SUBLANES - 1:SUBLANES, :], pltpu.roll(h, 1, 0))
    is_sample = i * tm >= n_prompt
    sample_start = jnp.where(row % t_new == 0, 1, 0)
    prompt_start = jnp.where((i * tm + row) % seq == 0, 1, 0)
    starts = jnp.where(is_sample, sample_start, prompt_start) == 1
    first = jnp.where(is_sample, first_ref[...], 0.0)
    xx = jnp.where(starts, first, prev) - h
    hb_ref[...] = h.astype(BF16)
    for j in range(6):
        m_ref[j] = (h + xx * mu_ref[j:j + 1, :]).astype(BF16)


def _token_shift_mix(h, first_rows, mu, n_prompt, seq, t_new, tm=128):
    n, d = h.shape
    assert n_prompt % tm == 0 and seq % tm == 0 and tm % t_new == 0 and tm % SUBLANES == 0
    per = tm // SUBLANES
    kern = functools.partial(_mix_kernel, n_prompt=n_prompt, seq=seq, t_new=t_new)
    return pl.pallas_call(
        kern,
        out_shape=(jax.ShapeDtypeStruct((n, d), BF16), jax.ShapeDtypeStruct((6, n, d), BF16)),
        grid=(n // tm,),
        in_specs=[pl.BlockSpec((tm, d), lambda i: (i, 0)),
                  pl.BlockSpec((SUBLANES, d), lambda i: (jnp.maximum(i * per - 1, 0), 0)),
                  pl.BlockSpec((tm, d), lambda i: (jnp.maximum(i - n_prompt // tm, 0), 0)),
                  pl.BlockSpec((6, d), lambda i: (0, 0))],
        out_specs=(pl.BlockSpec((tm, d), lambda i: (i, 0)), pl.BlockSpec((6, tm, d), lambda i: (0, i, 0))),
        compiler_params=_cparams(("parallel",)),
        name="token_shift_mix",
    )(h, h, first_rows, mu)


def _mm_kernel(*refs, n_extra, epilogue, w_transposed):
    a_ref, w_ref = refs[0], refs[1]
    extra = refs[2:2 + n_extra]
    o_ref = refs[2 + n_extra]
    dims = _NT if w_transposed else _NN
    acc = lax.dot_general(a_ref[...], w_ref[...].astype(BF16), (dims, ((), ())), preferred_element_type=F32)
    o_ref[...] = epilogue(acc, *extra).astype(o_ref.dtype)


def _matmul(a, w, *, col_off=0, width=None, tm, tn, epilogue=None, extra=(), extra_specs=(),
            out_dtype=F32, a_sel=None, w_transposed=False, name="matmul"):
    if a_sel is None:
        n, k = a.shape
        a_spec = pl.BlockSpec((tm, k), lambda j, i: (i, 0))
    else:
        _, n, k = a.shape
        a_spec = pl.BlockSpec((None, tm, k), lambda j, i: (a_sel, i, 0))
    n_cols = w.shape[0] if w_transposed else w.shape[1]
    width = n_cols - col_off if width is None else width
    assert col_off % tn == 0 and width % tn == 0 and n % tm == 0
    off = col_off // tn
    if epilogue is None:
        epilogue = lambda acc: acc
    kern = functools.partial(_mm_kernel, n_extra=len(extra), epilogue=epilogue, w_transposed=w_transposed)
    w_spec = (pl.BlockSpec((tn, k), lambda j, i: (j + off, 0)) if w_transposed
              else pl.BlockSpec((k, tn), lambda j, i: (0, j + off)))
    return pl.pallas_call(
        kern,
        out_shape=jax.ShapeDtypeStruct((n, width), out_dtype),
        grid=(width // tn, n // tm),
        in_specs=[a_spec, w_spec] + list(extra_specs),
        out_specs=pl.BlockSpec((tm, tn), lambda j, i: (i, j)),
        compiler_params=_cparams(("parallel", "parallel")),
        name=name,
    )(a, w, *extra)


def _qk_epilogue(acc, gain_ref, c_ref, a_ref, b_ref):
    c, a, b = c_ref[...], a_ref[...], b_ref[...]
    g = gain_ref[...]
    outs = []
    for hh in range(HEADS_PER_GROUP):
        x = acc[:, hh * HEAD_A:(hh + 1) * HEAD_A]
        ms = jnp.mean(x * x, axis=-1, keepdims=True)
        y = x * lax.rsqrt(ms + NORM_EPS) * g
        y = y * c + pltpu.roll(y, HEAD_A - ROPE_HALF, 1) * a + pltpu.roll(y, ROPE_HALF, 1) * b
        outs.append(y)
    return jnp.concatenate(outs, axis=1)


def _sigmoid_epilogue(acc):
    return jax.nn.sigmoid(acc)


def _valid_cols_epilogue(acc, *, valid):
    col = pl.program_id(0) * acc.shape[1] + lax.broadcasted_iota(jnp.int32, acc.shape, 1)
    return jnp.where(col < valid, acc, 0.0)


def _rope_tables(positions):
    n = positions.shape[0]
    inv_freq = ROPE_THETA ** (-jnp.arange(ROPE_HALF, dtype=F32) / ROPE_HALF)
    ang = positions.astype(F32)[:, None] * inv_freq[None, :]
    cos, sin = jnp.cos(ang), jnp.sin(ang)
    zeros = lambda w: jnp.zeros((n, w), F32)
    c = jnp.concatenate([cos, cos, jnp.ones((n, HEAD_A - ROPE_DIM), F32)], axis=1)
    a = jnp.concatenate([-sin, zeros(HEAD_A - ROPE_HALF)], axis=1)
    b = jnp.concatenate([zeros(ROPE_HALF), sin, zeros(HEAD_A - ROPE_DIM)], axis=1)
    return c, a, b


def _softmax_parts(s):
    m = jnp.max(s, axis=-1, keepdims=True)
    e = jnp.exp(s - m)
    denom = jnp.sum(e, axis=-1, keepdims=True)
    return e / denom, m + jnp.log(denom)


def _merge_three(outs, lses):
    m = jnp.maximum(jnp.maximum(lses[0], lses[1]), lses[2])
    es = [jnp.exp(l - m) for l in lses]
    tot = es[0] + es[1] + es[2]
    return (es[0] / tot) * outs[0] + (es[1] / tot) * outs[1] + (es[2] / tot) * outs[2]


def _prompt_attn_kernel(*refs, seq):
    ins, out_ref, o_s, l_s = refs[:3 * N_GROUPS], refs[3 * N_GROUPS], refs[3 * N_GROUPS + 1], refs[3 * N_GROUPS + 2]
    rows = lax.broadcasted_iota(jnp.int32, (ROW_BLOCK, ROW_BLOCK), 0)
    cols = lax.broadcasted_iota(jnp.int32, (ROW_BLOCK, ROW_BLOCK), 1)
    cur_ok = cols <= rows
    prev_ok = cols >= rows
    scale = HEAD_A ** -0.5
    nt = (((1,), (1,)), ((), ()))
    for g, d in enumerate(DILATIONS):
        q_ref, k_ref, v_ref = ins[3 * g:3 * g + 3]
        for c in range(d):
            for qb in range(seq // d // ROW_BLOCK):
                def class_rows(blk):
                    return pl.ds(blk * ROW_BLOCK, ROW_BLOCK) if d == 1 else pl.ds(c + blk * ROW_BLOCK * d, ROW_BLOCK, stride=d)

                sl = class_rows(qb)
                q = q_ref[sl, :].astype(BF16)
                v = v_ref[sl, :].astype(BF16)
                s = lax.dot_general(q, k_ref[sl, :].astype(BF16), nt, preferred_element_type=F32) * scale
                s = jnp.where(cur_ok, s, -jnp.inf)
                if qb > 0:
                    before = class_rows(qb - 1)
                    s_prev = lax.dot_general(q, k_ref[before, :].astype(BF16), nt, preferred_element_type=F32) * scale
                    s = jnp.concatenate([jnp.where(prev_ok, s_prev, -jnp.inf), s], axis=1)
                    v = jnp.concatenate([v_ref[before, :].astype(BF16), v], axis=0)
                p, lse = _softmax_parts(s)
                o_s[g, sl, :] = jnp.dot(p.astype(BF16), v, preferred_element_type=F32)
                l_s[g, sl, :] = jnp.broadcast_to(lse, (ROW_BLOCK, HEAD_A))
    out_ref[...] = _merge_three([o_s[g] for g in range(N_GROUPS)],
                                [l_s[g] for g in range(N_GROUPS)]).astype(out_ref.dtype)


def _prompt_attention(qk, v, batch, seq):
    heads = N_GROUPS * HEADS_PER_GROUP
    specs, args = [], []
    for g in range(N_GROUPS):
        specs += [pl.BlockSpec((seq, HEAD_A), lambda b, hh, g=g: (b, g * HEADS_PER_GROUP + hh)),
                  pl.BlockSpec((seq, HEAD_A), lambda b, hh, g=g: (b, heads + g * HEADS_PER_GROUP + hh)),
                  pl.BlockSpec((seq, HEAD_A), lambda b, hh, g=g: (b, g * HEADS_PER_GROUP + hh))]
        args += [qk, qk, v]
    return pl.pallas_call(
        functools.partial(_prompt_attn_kernel, seq=seq),
        out_shape=jax.ShapeDtypeStruct((batch * seq, GROUP_W), BF16),
        grid=(batch, HEADS_PER_GROUP),
        in_specs=specs,
        out_specs=pl.BlockSpec((seq, HEAD_A), lambda b, hh: (b, hh)),
        scratch_shapes=[pltpu.VMEM((N_GROUPS, seq, HEAD_A), F32), pltpu.VMEM((N_GROUPS, seq, HEAD_A), F32)],
        compiler_params=_cparams(("parallel", "parallel")),
        name="prompt_attn",
    )(*args)


SAMPLE_PAIR = 2
SAMPLE_ATTN_VMEM = 56 * 1024 * 1024


def _sample_attn_kernel(q_ref, k_ref, v_ref, c0_ref, c1_ref, c2_ref, out_ref, *, t_new):
    heads = HEADS_PER_GROUP
    scale = HEAD_A ** -0.5
    nq = heads * t_new
    qi = lax.broadcasted_iota(jnp.int32, (nq, 1), 0)
    q_h, q_t = qi // t_new, qi % t_new
    per_class = WINDOW_KEYS * heads
    out_rows = []
    for e in range(SAMPLE_PAIR):
        rows = slice(e * t_new, (e + 1) * t_new)
        outs, lses = [], []
        for g in range(N_GROUPS):
            def heads_to_rows(ref):
                return jnp.concatenate([ref[rows, (g * heads + h) * HEAD_A:(g * heads + h + 1) * HEAD_A]
                                        for h in range(heads)], axis=0)

            q = heads_to_rows(q_ref).astype(BF16)
            if g == 0:
                k_c = c0_ref[e, :, 0].reshape(per_class, HEAD_A)
                v_c = c0_ref[e, :, 1].reshape(per_class, HEAD_A)
                ncache = per_class
            else:
                cref = c1_ref if g == 1 else c2_ref
                k_c = jnp.concatenate([cref[e, :, c, 0].reshape(per_class, HEAD_A) for c in range(t_new)], axis=0)
                v_c = jnp.concatenate([cref[e, :, c, 1].reshape(per_class, HEAD_A) for c in range(t_new)], axis=0)
                ncache = t_new * per_class
            k_all = jnp.concatenate([k_c, heads_to_rows(k_ref)], axis=0).astype(BF16)
            v_all = jnp.concatenate([v_c, heads_to_rows(v_ref)], axis=0).astype(BF16)
            s = lax.dot_general(q, k_all, (((1,), (1,)), ((), ())), preferred_element_type=F32) * scale
            kj = lax.broadcasted_iota(jnp.int32, (1, ncache + nq), 1)
            is_new = kj >= ncache
            nj = kj - ncache
            k_h = jnp.where(is_new, nj // t_new, kj % heads)
            if g == 0:
                ok = jnp.where(is_new, q_t - nj % t_new, kj // heads - q_t) >= 0
            else:
                ok = jnp.where(is_new, nj % t_new, kj // per_class) == q_t
            s = jnp.where(jnp.logical_and(ok, k_h == q_h), s, -jnp.inf)
            p, lse = _softmax_parts(s)
            outs.append(jnp.dot(p.astype(BF16), v_all, preferred_element_type=F32))
            lses.append(lse)
        merged = _merge_three(outs, lses)
        out_rows.append(jnp.concatenate([merged[h * t_new:(h + 1) * t_new] for h in range(heads)], axis=1))
    out_ref[...] = jnp.concatenate(out_rows, axis=0).astype(out_ref.dtype)


def _sample_attention(qk, v, caches, row0, dec_batch, t_new):
    n = qk.shape[0]
    nrow = SAMPLE_PAIR * t_new
    assert nrow == 8 and row0 % nrow == 0 and dec_batch % SAMPLE_PAIR == 0
    rb0 = row0 // nrow
    tail = caches[0].shape[2:]
    c1 = caches[1].reshape((dec_batch, WINDOW_KEYS, DILATIONS[1]) + tail)
    c2 = caches[2].reshape((dec_batch, WINDOW_KEYS, DILATIONS[2]) + tail)
    assert DILATIONS[1] == t_new
    kern = functools.partial(_sample_attn_kernel, t_new=t_new)
    class_blk = (SAMPLE_PAIR, WINDOW_KEYS, t_new) + tail
    return pl.pallas_call(
        kern,
        out_shape=jax.ShapeDtypeStruct((dec_batch * t_new, GROUP_W), BF16),
        grid=(dec_batch // SAMPLE_PAIR,),
        in_specs=[pl.BlockSpec((nrow, A_WIDTH), lambda i: (rb0 + i, 0)),
                  pl.BlockSpec((nrow, A_WIDTH), lambda i: (rb0 + i, 1)),
                  pl.BlockSpec((nrow, A_WIDTH), lambda i: (rb0 + i, 0)),
                  pl.BlockSpec((SAMPLE_PAIR, WINDOW_KEYS) + tail, lambda i: (i, 0, 0, 0, 0)),
                  pl.BlockSpec(class_blk, lambda i: (i, 0, 0, 0, 0, 0)),
                  pl.BlockSpec(class_blk, lambda i: (i, 0, 0, 0, 0, 0))],
        out_specs=pl.BlockSpec((nrow, GROUP_W), lambda i: (i, 0)),
        compiler_params=_cparams(("parallel",), vmem=SAMPLE_ATTN_VMEM),
        name="sample_attn",
    )(qk, qk, v, caches[0], c1, c2)


def _cache_roll_kernel(c_ref, n_ref, o_ref):
    last = pl.program_id(0) == pl.num_programs(0) - 1
    o_ref[...] = jnp.where(last, n_ref[...], c_ref[...])


def _roll_cache(cache, new, t_new):
    nb, length = cache.shape[0], cache.shape[1]
    assert length % t_new == 0
    nblk = length // t_new
    blk = (nb, t_new) + cache.shape[2:]
    return pl.pallas_call(
        _cache_roll_kernel,
        out_shape=jax.ShapeDtypeStruct(cache.shape, cache.dtype),
        grid=(nblk,),
        in_specs=[pl.BlockSpec(blk, lambda j: (0, jnp.minimum(j + 1, nblk - 1), 0, 0, 0)),
                  pl.BlockSpec(blk, lambda j: (0, 0, 0, 0, 0))],
        out_specs=pl.BlockSpec(blk, lambda j: (0, j, 0, 0, 0)),
        compiler_params=_cparams(("parallel",)),
        name=f"cache_roll_w{length}",
    )(cache, new)


def _roll_caches(caches, news, t_new):
    return [_roll_cache(c, x.astype(c.dtype), t_new) for c, x in zip(caches, news)]


def _split_bf16(x):
    hi = x.astype(BF16)
    lo = (x - hi.astype(F32)).astype(BF16)
    return hi, lo


def _dot3(a, b, dims):
    ah, al = _split_bf16(a)
    bh, bl = _split_bf16(b)
    dn = (dims, ((), ()))
    out = lax.dot_general(ah, bh, dn, preferred_element_type=F32)
    out = out + lax.dot_general(ah, bl, dn, preferred_element_type=F32)
    return out + lax.dot_general(al, bh, dn, preferred_element_type=F32)


def _cumsum_rows(tri, x):
    hi = x.astype(BF16)
    rem = x - hi.astype(F32)
    mid = rem.astype(BF16)
    lo = (rem - mid.astype(F32)).astype(BF16)
    return (jnp.dot(tri, hi, preferred_element_type=F32) + jnp.dot(tri, mid, preferred_element_type=F32)
            + jnp.dot(tri, lo, preferred_element_type=F32))


_NN = ((1,), (0,))
_NT = ((1,), (1,))
_TN = ((0,), (0,))


def _segsum_mat():
    r = lax.broadcasted_iota(jnp.int32, (LANES, LANES), 0) // HEAD_B
    c = lax.broadcasted_iota(jnp.int32, (LANES, LANES), 1) // HEAD_B
    return (r == c).astype(BF16)


def _segsum(x, ones_bd):
    hi, lo = _split_bf16(x)
    return (jnp.dot(hi, ones_bd, preferred_element_type=F32) + jnp.dot(lo, ones_bd, preferred_element_type=F32))


def _rwkv_prep_kernel(k_ref, wl_ref, al_ref, gl_ref, w2_ref, a2_ref, g2_ref, vec_ref,
                      lw_ref, km_ref, na_ref, nb_ref, g_ref):
    w0, a0, k_k, k_a = vec_ref[0:1, :], vec_ref[1:2, :], vec_ref[2:3, :], vec_ref[3:4, :]
    w_lin = w0 + jnp.dot(jnp.tanh(wl_ref[...]).astype(BF16), w2_ref[...].astype(BF16), preferred_element_type=F32)
    w_log = -jax.nn.softplus(-w_lin) - 0.5
    lw_ref[...] = -jnp.exp(w_log)
    a = jax.nn.sigmoid(a0 + jnp.dot(al_ref[...].astype(BF16), a2_ref[...].astype(BF16), preferred_element_type=F32))
    g_ref[...] = jnp.dot(jax.nn.sigmoid(gl_ref[...]).astype(BF16), g2_ref[...].astype(BF16), preferred_element_type=F32)
    k = k_ref[...]
    km_ref[...] = k * (1.0 + (a - 1.0) * k_a)
    kk = k * k_k
    ones_bd = _segsum_mat()
    for s in range(B_WIDTH // LANES):
        sl = slice(s * LANES, (s + 1) * LANES)
        kks = kk[:, sl]
        nrm = jnp.maximum(jnp.sqrt(_segsum(kks * kks, ones_bd)), 1e-12)
        kkn = kks / nrm
        na_ref[:, sl] = -kkn
        nb_ref[:, sl] = kkn * a[:, sl]


def _rwkv_prep(k, w_low, a_low, g_low, w2, a2, g2_pad, vecs, tm=256):
    n = k.shape[0]
    wide = pl.BlockSpec((tm, B_WIDTH), lambda i: (i, 0))
    full = lambda shape: pl.BlockSpec(shape, lambda i: (0, 0))
    sds = jax.ShapeDtypeStruct((n, B_WIDTH), F32)
    return pl.pallas_call(
        _rwkv_prep_kernel,
        out_shape=(sds,) * 5,
        grid=(n // tm,),
        in_specs=[wide, pl.BlockSpec((tm, DECAY_LORA), lambda i: (i, 0)), pl.BlockSpec((tm, AAA_LORA), lambda i: (i, 0)),
                  pl.BlockSpec((tm, GATE_LORA_PAD), lambda i: (i, 0)),
                  full((DECAY_LORA, B_WIDTH)), full((AAA_LORA, B_WIDTH)), full((GATE_LORA_PAD, B_WIDTH)),
                  full((4, B_WIDTH))],
        out_specs=(wide,) * 5,
        compiler_params=_cparams(("parallel",)),
        name="rwkv_prep",
    )(k, w_low, a_low, g_low, w2, a2, g2_pad, vecs)


def _each(fn, *lists):
    return [fn(*args) for args in zip(*lists)]


def _dot1(a, b, dims):
    return lax.dot_general(a.astype(BF16), b.astype(BF16), (dims, ((), ())), preferred_element_type=F32)


def _mm_each(xs, ys, dims):
    return _each(lambda x, y: _dot1(x, y, dims), xs, ys)


def _tri_inverse(a_bds, c):
    n = 2 * c
    r = lax.broadcasted_iota(jnp.int32, (n, n), 0)
    q = lax.broadcasted_iota(jnp.int32, (n, n), 1)
    eye = (r == q).astype(F32)
    base = 8
    in_base = r // base == q // base
    a0 = [jnp.where(in_base, a, 0.0) for a in a_bds]
    a2 = _mm_each(a0, a0, _NN)
    a4 = _mm_each(a2, a2, _NN)
    x = [eye + a for a in a0]
    x = _each(jnp.add, x, _mm_each(x, a2, _NN))
    x = _each(jnp.add, x, _mm_each(x, a4, _NN))
    size = base
    while size < c:
        below = jnp.logical_and(r // (2 * size) == q // (2 * size), r // size == q // size + 1)
        off = [jnp.where(below, a, 0.0) for a in a_bds]
        x = _each(jnp.add, x, _mm_each(_mm_each(x, off, _NN), x, _NN))
        size *= 2
    return x


def _rwkv_chunk(chains, c):
    n = 2 * c
    lane = lax.broadcasted_iota(jnp.int32, (c, LANES), 1)
    m0 = (lane < HEAD_B).astype(F32)
    m1 = 1.0 - m0
    stack = lambda x: jnp.concatenate([x * m0, x * m1], axis=0)
    r, lw, k, v, a, b, s_bd = (list(col) for col in zip(*chains))

    tr = lax.broadcasted_iota(jnp.int32, (c, c), 0)
    tc = lax.broadcasted_iota(jnp.int32, (c, c), 1)
    tri = (tr >= tc).astype(BF16)
    cl = [_cumsum_rows(tri, x) for x in lw]
    cl_end = [x[c - 1:c, :] for x in cl]
    r_t = _each(lambda x, d: stack(x * jnp.exp(d)), r, cl)
    a_t = _each(lambda x, d, w: stack(x * jnp.exp(d - w)), a, cl, lw)
    b_t = _each(lambda x, d: stack(x * jnp.exp(-d)), b, cl)
    k_t = _each(lambda x, d: stack(x * jnp.exp(-d)), k, cl)
    bk_e = _each(lambda x, y, d, e: jnp.concatenate([stack(x * jnp.exp(e - d)), stack(y * jnp.exp(e - d))], axis=0),
                 b, k, cl, cl_end)
    v_s = [stack(x) for x in v]

    rr = lax.broadcasted_iota(jnp.int32, (n, n), 0)
    qq = lax.broadcasted_iota(jnp.int32, (n, n), 1)
    same = rr // c == qq // c
    strict = jnp.logical_and(same, rr > qq)
    incl = jnp.logical_and(same, rr >= qq)
    ar = _each(lambda x, y: jnp.concatenate([x, y], axis=0), a_t, r_t)
    gb = _mm_each(ar, b_t, _NT)
    gk = _mm_each(ar, k_t, _NT)
    ab = [jnp.where(strict, g[:n], 0.0) for g in gb]
    rb = [jnp.where(incl, g[n:], 0.0) for g in gb]
    ak = [jnp.where(strict, g[:n], 0.0) for g in gk]
    rk = [jnp.where(incl, g[n:], 0.0) for g in gk]

    us = _mm_each(ar, s_bd, _NT)
    akv = _mm_each(ak, v_s, _NN)
    rkv = _mm_each(rk, v_s, _NN)
    t_inv = _tri_inverse(ab, c)
    u = _mm_each(t_inv, _each(lambda x, y: x[:n] + y, us, akv), _NN)
    rbu = _mm_each(rb, u, _NN)
    y_s = _each(lambda x, y, z: x[n:] + y + z, us, rbu, rkv)
    upd = _mm_each(_each(lambda x, y: jnp.concatenate([x, y], axis=0), u, v_s), bk_e, _TN)
    s_new = _each(lambda s, e, d: s * jnp.exp(e) + d, s_bd, cl_end, upd)
    return [(y[:c] + y[c:], s) for y, s in zip(y_s, s_new)]


RWKV_PAIRS = 16


def _rwkv_prompt_kernel(r_ref, lw_ref, k_ref, v_ref, a_ref, b_ref, y_ref, s_ref, s_acc):
    ci = pl.program_id(2)

    @pl.when(ci == 0)
    def _():
        s_acc[...] = jnp.zeros_like(s_acc)

    lanes = [slice(p * LANES, (p + 1) * LANES) for p in range(RWKV_PAIRS)]
    chains = [(r_ref[:, sl], lw_ref[:, sl], k_ref[:, sl], v_ref[:, sl], a_ref[:, sl], b_ref[:, sl], s_acc[p])
              for p, sl in enumerate(lanes)]
    res = _rwkv_chunk(chains, CHUNK)
    for p, sl in enumerate(lanes):
        y_ref[:, sl] = res[p][0]
        s_acc[p] = res[p][1]

    @pl.when(ci == pl.num_programs(2) - 1)
    def _():
        for p in range(RWKV_PAIRS):
            _store_pair_state(s_ref, p, s_acc[p])


def _pair_state(s_ref, p):
    zero = jnp.zeros((HEAD_B, HEAD_B), F32)
    top = jnp.concatenate([s_ref[2 * p], zero], axis=1)
    bot = jnp.concatenate([zero, s_ref[2 * p + 1]], axis=1)
    return jnp.concatenate([top, bot], axis=0)


def _store_pair_state(s_ref, p, s_bd):
    s_ref[2 * p] = s_bd[:HEAD_B, :HEAD_B]
    s_ref[2 * p + 1] = s_bd[HEAD_B:, HEAD_B:]


def _rwkv_prompt(r, lw, k, v, a, b, batch, seq):
    npair = B_WIDTH // LANES
    nchunk = seq // CHUNK
    wide = RWKV_PAIRS * LANES
    spec = pl.BlockSpec((CHUNK, wide), lambda bi, p, ci: (bi * nchunk + ci, p))
    return pl.pallas_call(
        _rwkv_prompt_kernel,
        out_shape=(jax.ShapeDtypeStruct((batch * seq, B_WIDTH), F32),
                   jax.ShapeDtypeStruct((batch, 2 * npair, HEAD_B, HEAD_B), F32)),
        grid=(batch, npair // RWKV_PAIRS, nchunk),
        in_specs=[spec] * 6,
        out_specs=(spec, pl.BlockSpec((None, 2 * RWKV_PAIRS, HEAD_B, HEAD_B), lambda bi, p, ci: (bi, p, 0, 0))),
        scratch_shapes=[pltpu.VMEM((RWKV_PAIRS, LANES, LANES), F32)],
        compiler_params=_cparams(("parallel", "parallel", "arbitrary")),
        name="rwkv_prompt",
    )(r, lw, k, v, a, b)


def _rwkv_sample_kernel(r_ref, lw_ref, k_ref, v_ref, a_ref, b_ref, s0_ref, y_ref, s_ref, *, t_new):
    c = SAMPLE_CHUNK
    row = lax.broadcasted_iota(jnp.int32, (c, LANES), 0)
    def sel(ref, sl, e):
        x = ref[:, sl]
        if e:
            x = pltpu.roll(x, c - e * t_new, 0)
        return jnp.where(row < t_new, x, 0.0)

    ids = [(p, e) for p in range(RWKV_PAIRS) for e in range(SAMPLE_PAIR)]
    chains = []
    for p, e in ids:
        sl = slice(p * LANES, (p + 1) * LANES)
        chains.append(tuple(sel(ref, sl, e) for ref in (r_ref, lw_ref, k_ref, v_ref, a_ref, b_ref))
                      + (_pair_state(s0_ref.at[e], p),))
    res = dict(zip(ids, _rwkv_chunk(chains, c)))
    for p in range(RWKV_PAIRS):
        y_all = res[(p, 0)][0]
        for e in range(1, SAMPLE_PAIR):
            y_all = jnp.where(row // t_new == e, pltpu.roll(res[(p, e)][0], e * t_new, 0), y_all)
        y_ref[:, p * LANES:(p + 1) * LANES] = y_all
        for e in range(SAMPLE_PAIR):
            _store_pair_state(s_ref.at[e], p, res[(p, e)][1])


def _rwkv_sample(r, lw, k, v, a, b, s0, row0, dec_batch, t_new):
    npair = B_WIDTH // LANES
    nrow = SAMPLE_PAIR * t_new
    rb0 = row0 // nrow
    wide = RWKV_PAIRS * LANES
    spec = pl.BlockSpec((nrow, wide), lambda i, p: (rb0 + i, p))
    s_spec = pl.BlockSpec((SAMPLE_PAIR, 2 * RWKV_PAIRS, HEAD_B, HEAD_B), lambda i, p: (i, p, 0, 0))
    return pl.pallas_call(
        functools.partial(_rwkv_sample_kernel, t_new=t_new),
        out_shape=(jax.ShapeDtypeStruct((dec_batch * t_new, B_WIDTH), F32),
                   jax.ShapeDtypeStruct(s0.shape, F32)),
        grid=(dec_batch // SAMPLE_PAIR, npair // RWKV_PAIRS),
        in_specs=[spec] * 6 + [s_spec],
        out_specs=(pl.BlockSpec((nrow, wide), lambda i, p: (i, p)), s_spec),
        compiler_params=_cparams(("parallel", "parallel")),
        name="rwkv_sample",
    )(r, lw, k, v, a, b, s0)


def _rwkv_post_kernel(y_ref, r_ref, k_ref, v_ref, g_ref, vec_ref, o_ref):
    ones_bd = _segsum_mat()
    inv = 1.0 / HEAD_B
    for s in range(B_WIDTH // LANES):
        sl = slice(s * LANES, (s + 1) * LANES)
        y = y_ref[:, sl]
        mean = _segsum(y, ones_bd) * inv
        dlt = y - mean
        var = _segsum(dlt * dlt, ones_bd) * inv
        yn = dlt * lax.rsqrt(var + GN_EPS) * vec_ref[0:1, sl] + vec_ref[1:2, sl]
        bonus = _segsum(r_ref[:, sl] * k_ref[:, sl] * vec_ref[2:3, sl], ones_bd) * v_ref[:, sl]
        o_ref[:, sl] = ((yn + bonus) * g_ref[:, sl]).astype(o_ref.dtype)


def _rwkv_post(y, r, k_mod, v, g, vecs, tm=256):
    n = y.shape[0]
    wide = pl.BlockSpec((tm, B_WIDTH), lambda i: (i, 0))
    return pl.pallas_call(
        _rwkv_post_kernel,
        out_shape=jax.ShapeDtypeStruct((n, B_WIDTH), BF16),
        grid=(n // tm,),
        in_specs=[wide] * 5 + [pl.BlockSpec((3, B_WIDTH), lambda i: (0, 0))],
        out_specs=wide,
        compiler_params=_cparams(("parallel",)),
        name="rwkv_post",
    )(y, r, k_mod, v, g, vecs)


def _read_kernel(ao_ref, bo_ref, wa_ref, wb_ref, ga_ref, gb_ref, o_ref):
    ra = jnp.dot(ao_ref[...], wa_ref[...].astype(BF16), preferred_element_type=F32)
    rb = jnp.dot(bo_ref[...], wb_ref[...].astype(BF16), preferred_element_type=F32)
    o_ref[...] = (ga_ref[...] * ra + gb_ref[...] * rb).astype(o_ref.dtype)


def _gated_read(a_out, b_out, w_read_a, w_read_b, gates, tm, tn=512):
    n = a_out.shape[0]
    ncol = D_MODEL // tn
    return pl.pallas_call(
        _read_kernel,
        out_shape=jax.ShapeDtypeStruct((n, D_MODEL), BF16),
        grid=(ncol, n // tm),
        in_specs=[pl.BlockSpec((tm, GROUP_W), lambda j, i: (i, 0)),
                  pl.BlockSpec((tm, B_WIDTH), lambda j, i: (i, 0)),
                  pl.BlockSpec((GROUP_W, tn), lambda j, i: (0, j)),
                  pl.BlockSpec((B_WIDTH, tn), lambda j, i: (0, j)),
                  pl.BlockSpec((tm, tn), lambda j, i: (i, j)),
                  pl.BlockSpec((tm, tn), lambda j, i: (i, j + ncol))],
        out_specs=pl.BlockSpec((tm, tn), lambda j, i: (i, j)),
        compiler_params=_cparams(("parallel", "parallel")),
        name="gated_read",
    )(a_out, b_out, w_read_a, w_read_b, gates, gates)


def _residual_epilogue(acc, x_ref):
    return x_ref[...] + acc


def _pack_halves(h):
    half = h.shape[1] // 2
    top = pltpu.bitcast(h[:, :half].astype(BF16).astype(F32), jnp.uint32)
    bot = pltpu.bitcast(h[:, half:].astype(BF16).astype(F32), jnp.uint32)
    return top | (bot >> 16)


def _unpack_halves(p):
    top = pltpu.bitcast(p & jnp.uint32(0xFFFF0000), F32).astype(BF16)
    bot = pltpu.bitcast(p << 16, F32).astype(BF16)
    return top, bot


def _ffn_norm_router_kernel(u_ref, g_ref, wr_ref, hb_ref, lg_ref):
    x = u_ref[...]
    ms = jnp.mean(x * x, axis=-1, keepdims=True)
    h = x * lax.rsqrt(ms + NORM_EPS) * g_ref[...]
    hb_ref[...] = _pack_halves(h)
    lg_ref[...] = _dot3(h, wr_ref[...], _NN)


def _ffn_norm_router(u, gain, w_router, tm=256):
    n, d = u.shape
    return pl.pallas_call(
        _ffn_norm_router_kernel,
        out_shape=(jax.ShapeDtypeStruct((n, d // 2), jnp.uint32), jax.ShapeDtypeStruct((n, ROUTER_PAD), F32)),
        grid=(n // tm,),
        in_specs=[pl.BlockSpec((tm, d), lambda i: (i, 0)), pl.BlockSpec((1, d), lambda i: (0, 0)),
                  pl.BlockSpec((d, ROUTER_PAD), lambda i: (0, 0))],
        out_specs=(pl.BlockSpec((tm, d // 2), lambda i: (i, 0)), pl.BlockSpec((tm, ROUTER_PAD), lambda i: (i, 0))),
        compiler_params=_cparams(("parallel",)),
        name="ffn_norm_router",
    )(u, gain.reshape(1, d), w_router)


def _route(logits, group_b, expert_b):
    n = logits.shape[0]
    group_logits = logits[:, :N_EXPERT_GROUPS] + group_b.astype(F32)
    group = jnp.argmax(group_logits, axis=-1).astype(jnp.int32)
    p_group = jnp.take_along_axis(jax.nn.softmax(group_logits, axis=-1), group[:, None], axis=-1)
    expert_logits = (logits[:, N_EXPERT_GROUPS:N_EXPERT_GROUPS + N_EXPERTS] + expert_b.astype(F32)).reshape(
        n, N_EXPERT_GROUPS, EXPERTS_PER_GROUP)
    in_group = jnp.take_along_axis(expert_logits, group[:, None, None], axis=1)[:, 0]
    top_val, top_idx = lax.top_k(in_group, TOP_K)
    gate = p_group * jax.nn.softmax(top_val, axis=-1)
    expert_id = group[:, None] * EXPERTS_PER_GROUP + top_idx.astype(jnp.int32)
    return expert_id, gate


FFN_CHUNKS = 1
FFN_VMEM = 58 * 1024 * 1024
WEIGHT_COPY_PARTS = 4


def _moe_plan(expert_id):
    n_assign = expert_id.size
    n_blocks = n_assign // ROW_BLOCK + N_EXPERTS
    n_rows = n_blocks * ROW_BLOCK
    max_items = FFN_CHUNKS * n_blocks
    i32 = jnp.int32
    e_flat = expert_id.reshape(n_assign)
    onehot = (e_flat[:, None] == jnp.arange(N_EXPERTS, dtype=i32)[None, :]).astype(i32)
    running = jnp.cumsum(onehot, axis=0)
    rank = jnp.sum(onehot * (running - 1), axis=1)
    counts = running[-1]
    nblk = (counts + ROW_BLOCK - 1) // ROW_BLOCK
    blk_start = jnp.cumsum(nblk) - nblk
    pos = (blk_start[e_flat] * ROW_BLOCK + rank).astype(i32)
    row_src = jnp.zeros((n_rows,), i32).at[pos].set(jnp.arange(n_assign, dtype=i32) // TOP_K)
    item_cnt = FFN_CHUNKS * nblk
    item_end = jnp.cumsum(item_cnt)
    item_start = item_end - item_cnt
    total = item_end[-1]
    idx = jnp.arange(max_items, dtype=i32)
    ic = jnp.minimum(idx, total - 1)
    it_e = jnp.minimum(jnp.sum((item_end[None, :] <= ic[:, None]).astype(i32), axis=1), N_EXPERTS - 1)
    within = ic - item_start[it_e]
    nb = jnp.maximum(nblk[it_e], 1)
    it_wc = within // nb
    valid = idx < total
    first = jnp.logical_and(valid, within % nb == 0)
    tail = idx - total
    it_blk = jnp.where(valid, blk_start[it_e] + within % nb, total // FFN_CHUNKS + tail // FFN_CHUNKS)
    it_c = jnp.where(valid, it_wc, tail % FFN_CHUNKS)
    it_slot = jnp.maximum(jnp.cumsum(first.astype(i32)) - 1, 0) % 2
    owner = jnp.where(nblk > 0, jnp.arange(N_EXPERTS, dtype=i32), N_EXPERTS)
    later = jnp.concatenate([lax.cummin(owner[::-1])[::-1][1:], jnp.full((1,), N_EXPERTS, i32)])
    it_next = jnp.where(later < N_EXPERTS, later, -1)[it_e]
    return dict(pos=pos, row_src=row_src, it_e=it_e, it_wc=it_wc.astype(i32), it_c=it_c.astype(i32),
                it_blk=it_blk.astype(i32), it_valid=valid.astype(i32), it_first=first.astype(i32),
                it_slot=it_slot.astype(i32), it_next=it_next.astype(i32), n_rows=n_rows, max_items=max_items)


GATHER_ROWS = 128


def _dispatch_kernel(src_ref, x_hbm, o_ref, buf, sem):
    i = pl.program_id(0)
    slot = i % 2

    def row_copy(slot_, j, src_row):
        return pltpu.make_async_copy(x_hbm.at[pl.ds(src_row, 1)], buf.at[slot_, pl.ds(j, 1)], sem.at[slot_])

    def issue(block, slot_):
        base = block * GATHER_ROWS

        def body(j, carry):
            row_copy(slot_, j, src_ref[base + j]).start()
            return carry

        lax.fori_loop(0, GATHER_ROWS, body, 0, unroll=8)

    @pl.when(i == 0)
    def _():
        issue(0, 0)

    @pl.when(i + 1 < pl.num_programs(0))
    def _():
        issue(i + 1, 1 - slot)

    for j in range(GATHER_ROWS):
        row_copy(slot, j, 0).wait()
    o_ref[...] = buf[slot]


def _dispatch_rows(x, row_src, n_rows):
    d = x.shape[1]
    return pl.pallas_call(
        _dispatch_kernel,
        out_shape=jax.ShapeDtypeStruct((n_rows, d), x.dtype),
        grid_spec=pltpu.PrefetchScalarGridSpec(
            num_scalar_prefetch=1,
            grid=(n_rows // GATHER_ROWS,),
            in_specs=[pl.BlockSpec(memory_space=pl.ANY)],
            out_specs=pl.BlockSpec((GATHER_ROWS, d), lambda i, src: (i, 0)),
            scratch_shapes=[pltpu.VMEM((2, GATHER_ROWS, d), x.dtype), pltpu.SemaphoreType.DMA((2,))],
        ),
        compiler_params=_cparams(("arbitrary",)),
        name="moe_dispatch",
    )(row_src, x)


def _ffn_kernel(e_ref, blk_ref, valid_ref, first_ref, slot_ref, next_ref, *refs, mode):
    if mode == "up":
        x_ref, g_ref, w_hbm, o_ref, w_stage, w_bf, sem = refs
    else:
        x_ref, w_hbm, o_ref, w_stage, w_bf, sem = refs
    i = pl.program_id(0)

    part = w_stage.shape[1] // WEIGHT_COPY_PARTS

    def weight_copies(expert, slot):
        return [pltpu.make_async_copy(w_hbm.at[expert, pl.ds(j * part, part)], w_stage.at[slot, pl.ds(j * part, part)],
                                      sem.at[slot, j]) for j in range(WEIGHT_COPY_PARTS)]

    @pl.when(i == 0)
    def _():
        for cp in weight_copies(e_ref[0], 0):
            cp.start()

    @pl.when(first_ref[i] == 1)
    def _():
        slot = slot_ref[i]
        for cp in weight_copies(e_ref[i], slot):
            cp.wait()
        w_bf[...] = w_stage[slot].astype(BF16)

        @pl.when(next_ref[i] >= 0)
        def _():
            for cp in weight_copies(next_ref[i], 1 - slot):
                cp.start()

    @pl.when(valid_ref[i] == 0)
    def _():
        o_ref[...] = jnp.zeros_like(o_ref)

    @pl.when(valid_ref[i] == 1)
    def _():
        if mode == "down":
            acc = jnp.dot(x_ref[...], w_bf[...], preferred_element_type=F32)
        else:
            top, bot = _unpack_halves(x_ref[...])
            half = top.shape[1]
            acc = (jnp.dot(top, w_bf[:half, :], preferred_element_type=F32)
                   + jnp.dot(bot, w_bf[half:, :], preferred_element_type=F32))
        if mode == "up":
            acc = jax.nn.silu(g_ref[...]) * acc
        o_ref[...] = acc.astype(o_ref.dtype)


def _expert_matmul(mode, rows, w, plan, extra=None):
    n_rows, row_w = rows.shape
    k, out_w = w.shape[1], w.shape[2]
    assert FFN_CHUNKS == 1
    prefetch = (plan["it_e"], plan["it_blk"], plan["it_valid"], plan["it_first"], plan["it_slot"], plan["it_next"])
    row_spec = lambda width: pl.BlockSpec((ROW_BLOCK, width), lambda i, e, b, v, f, s, nx: (b[i], 0))
    in_specs = [row_spec(row_w)] + ([row_spec(out_w)] if mode == "up" else [])
    in_specs.append(pl.BlockSpec(memory_space=pl.ANY))
    args = (rows,) + ((extra,) if mode == "up" else ()) + (w,)
    return pl.pallas_call(
        functools.partial(_ffn_kernel, mode=mode),
        out_shape=jax.ShapeDtypeStruct((n_rows, out_w), BF16 if mode == "up" else F32),
        grid_spec=pltpu.PrefetchScalarGridSpec(
            num_scalar_prefetch=len(prefetch),
            grid=(plan["max_items"],),
            in_specs=in_specs,
            out_specs=row_spec(out_w),
            scratch_shapes=[pltpu.VMEM((2, k, out_w), F32), pltpu.VMEM((k, out_w), BF16),
                            pltpu.SemaphoreType.DMA((2, WEIGHT_COPY_PARTS))],
        ),
        compiler_params=_cparams(("arbitrary",), vmem=FFN_VMEM),
        name=f"expert_ffn_{mode}",
    )(*prefetch, *args)


def _expert_ffn(x_sorted, plan, w_gate, w_up, w_down):
    gate = _expert_matmul("gate", x_sorted, w_gate, plan)
    hid = _expert_matmul("up", x_sorted, w_up, plan, extra=gate)
    return _expert_matmul("down", hid, w_down, plan)


COMBINE_ROWS = 64


def _combine_kernel(pos_ref, u_ref, g_ref, y_hbm, o_ref, buf, sem):
    i = pl.program_id(0)
    slot = i % 2

    def row_copy(slot_, s, j, src_row):
        return pltpu.make_async_copy(y_hbm.at[pl.ds(src_row, 1)], buf.at[slot_, s, pl.ds(j, 1)], sem.at[slot_])

    def issue(tile, slot_):
        base = tile * COMBINE_ROWS

        def body(j, carry):
            for s in range(TOP_K):
                row_copy(slot_, s, j, pos_ref[(base + j) * TOP_K + s]).start()
            return carry

        lax.fori_loop(0, COMBINE_ROWS, body, 0, unroll=8)

    @pl.when(i == 0)
    def _():
        issue(0, 0)

    @pl.when(i + 1 < pl.num_programs(0))
    def _():
        issue(i + 1, 1 - slot)

    for j in range(COMBINE_ROWS):
        for s in range(TOP_K):
            row_copy(slot, s, j, 0).wait()
    g = g_ref[...]
    moe = buf[slot, 0] * g[:, 0:1] + buf[slot, 1] * g[:, 1:2]
    o_ref[...] = u_ref[...] + moe


def _combine(u, gate, y_sorted, pos):
    n, d = u.shape
    return pl.pallas_call(
        _combine_kernel,
        out_shape=jax.ShapeDtypeStruct((n, d), F32),
        grid_spec=pltpu.PrefetchScalarGridSpec(
            num_scalar_prefetch=1,
            grid=(n // COMBINE_ROWS,),
            in_specs=[pl.BlockSpec((COMBINE_ROWS, d), lambda i, p: (i, 0)),
                      pl.BlockSpec((COMBINE_ROWS, TOP_K), lambda i, p: (i, 0)),
                      pl.BlockSpec(memory_space=pl.ANY)],
            out_specs=pl.BlockSpec((COMBINE_ROWS, d), lambda i, p: (i, 0)),
            scratch_shapes=[pltpu.VMEM((2, TOP_K, COMBINE_ROWS, d), F32), pltpu.SemaphoreType.DMA((2,))],
        ),
        compiler_params=_cparams(("arbitrary",)),
        name="moe_combine",
    )(pos, u, gate, y_sorted)


def _pick_tm(n):
    for tm in (1088, 1024, 544, 512, 272, 256, 128):
        if n % tm == 0:
            return tm
    raise ValueError(f"row count {n} is not a multiple of 128")


def kernel(x_prompt, x_sample, cache_kv_w128, cache_kv_w512, cache_kv_w2048, state_shift, state_wkv, norm_mix, w_in, q_norm, k_norm, mu_shift, rwkv_w0, rwkv_w2, rwkv_a0, rwkv_a2, rwkv_g2, rwkv_k_k, rwkv_k_a, rwkv_r_k, rwkv_ln_w, rwkv_ln_b, w_read_a, w_read_b, w_o, norm_ffn, router_group_w, router_group_b, router_expert_w, router_expert_b, expert_w_gate, expert_w_up, expert_w_down):
    batch, seq, d = x_prompt.shape
    dec_batch, t_new, _ = x_sample.shape
    n_p, n_s = batch * seq, dec_batch * t_new
    n = n_p + n_s
    past = cache_kv_w2048.shape[1]
    assert d == D_MODEL and seq % (DILATIONS[2] * ROW_BLOCK) == 0 and seq % CHUNK == 0
    assert cache_kv_w128.shape[1] == 128 and cache_kv_w512.shape[1] == 512 and past == 2048
    tm = _pick_tm(n)

    x = jnp.concatenate([x_prompt.reshape(n_p, d), x_sample.reshape(n_s, d)], axis=0)
    h = _rmsnorm(x, norm_mix)
    shift_p = h[seq - 1:n_p:seq]
    shift_s = h[n_p + t_new - 1::t_new]
    first_rows = jnp.broadcast_to(state_shift[:, None].astype(F32), (dec_batch, t_new, d)).reshape(n_s, d)
    hb, mixes = _token_shift_mix(h, first_rows, mu_shift, n_p, seq, t_new)

    positions = jnp.concatenate([jnp.tile(jnp.arange(seq, dtype=jnp.int32), batch),
                                 jnp.tile(past + jnp.arange(t_new, dtype=jnp.int32), dec_batch)])
    rope_c, rope_a, rope_b = _rope_tables(positions)
    gains = jnp.stack([q_norm, k_norm]).reshape(2, 1, HEAD_A).astype(F32)
    tn = GROUP_W
    row_spec = pl.BlockSpec((tm, HEAD_A), lambda j, i: (i, 0))
    w_in_t = w_in.T
    proj = functools.partial(_matmul, w=w_in_t, w_transposed=True, tm=tm)
    qk = proj(hb, col_off=0, width=2 * A_WIDTH, tn=tn, epilogue=_qk_epilogue,
              extra=(gains, rope_c, rope_a, rope_b),
              extra_specs=(pl.BlockSpec((None, 1, HEAD_A), lambda j, i: (j // N_GROUPS, 0, 0)),
                           row_spec, row_spec, row_spec), name="proj_qk")
    v_a = proj(hb, col_off=2 * A_WIDTH, width=A_WIDTH, tn=tn, name="proj_v")
    gates = proj(hb, col_off=3 * A_WIDTH, width=2 * D_MODEL, tn=tn, epilogue=_sigmoid_epilogue, name="proj_gates")
    r_b = proj(mixes, a_sel=0, col_off=X_COLS, width=B_WIDTH, tn=tn, name="proj_r")
    k_b = proj(mixes, a_sel=1, col_off=X_COLS + B_WIDTH, width=B_WIDTH, tn=tn, name="proj_k")
    v_b = proj(mixes, a_sel=2, col_off=X_COLS + 2 * B_WIDTH, width=B_WIDTH, tn=tn, name="proj_vb")
    lora0 = X_COLS + 3 * B_WIDTH
    w_low = proj(mixes, a_sel=3, col_off=lora0, width=DECAY_LORA, tn=DECAY_LORA, name="proj_wlow")
    a_low = proj(mixes, a_sel=4, col_off=lora0 + DECAY_LORA, width=AAA_LORA, tn=AAA_LORA, name="proj_alow")
    g_low = proj(mixes, a_sel=5, col_off=lora0 + DECAY_LORA + AAA_LORA, width=GATE_LORA_PAD, tn=DECAY_LORA,
                 epilogue=functools.partial(_valid_cols_epilogue, valid=GATE_LORA), name="proj_glow")

    a_out_p = _prompt_attention(qk, v_a, batch, seq)
    caches = (cache_kv_w128, cache_kv_w512, cache_kv_w2048)
    a_out_s = _sample_attention(qk, v_a, caches, n_p, dec_batch, t_new)
    a_out = jnp.concatenate([a_out_p, a_out_s], axis=0)

    def kv_rows(g, rows):
        kk = qk[rows, A_WIDTH + g * GROUP_W:A_WIDTH + (g + 1) * GROUP_W]
        vv = v_a[rows, g * GROUP_W:(g + 1) * GROUP_W]
        return jnp.stack([kk.reshape(-1, HEADS_PER_GROUP, HEAD_A), vv.reshape(-1, HEADS_PER_GROUP, HEAD_A)], axis=1)

    kv_prompt = []
    for g in range(N_GROUPS):
        keep = min(caches[g].shape[1], seq)
        full = kv_rows(g, slice(0, n_p)).reshape(batch, seq, 2, HEADS_PER_GROUP, HEAD_A)
        kv_prompt.append(full[:, seq - keep:])
    news = [kv_rows(g, slice(n_p, n)).reshape(dec_batch, t_new, 2, HEADS_PER_GROUP, HEAD_A) for g in range(N_GROUPS)]
    kv_sample = _roll_caches(caches, news, t_new)

    g2_pad = jnp.pad(rwkv_g2, ((0, GATE_LORA_PAD - GATE_LORA), (0, 0)))
    vec_prep = jnp.stack([rwkv_w0, rwkv_a0, rwkv_k_k, rwkv_k_a]).astype(F32)
    lw, k_mod, neg_kk, kk_a, g_out = _rwkv_prep(k_b, w_low, a_low, g_low, rwkv_w2, rwkv_a2, g2_pad, vec_prep)
    y_p, s_p = _rwkv_prompt(r_b, lw, k_mod, v_b, neg_kk, kk_a, batch, seq)
    y_s, s_s = _rwkv_sample(r_b, lw, k_mod, v_b, neg_kk, kk_a, state_wkv.astype(F32), n_p, dec_batch, t_new)
    y_b = jnp.concatenate([y_p, y_s], axis=0)
    vec_post = jnp.stack([rwkv_ln_w, rwkv_ln_b, rwkv_r_k.reshape(B_WIDTH)]).astype(F32)
    b_out = _rwkv_post(y_b, r_b, k_mod, v_b, g_out, vec_post)

    merged = _gated_read(a_out, b_out, w_read_a, w_read_b, gates, tm)
    u = _matmul(merged, w_o, tm=tm, tn=tn, epilogue=_residual_epilogue, extra=(x,),
                extra_specs=(pl.BlockSpec((tm, tn), lambda j, i: (i, j)),), name="proj_out")
    w_router = jnp.pad(jnp.concatenate([router_group_w, router_expert_w], axis=1),
                       ((0, 0), (0, ROUTER_PAD - N_EXPERT_GROUPS - N_EXPERTS)))
    hn, logits = _ffn_norm_router(u, norm_ffn, w_router)
    expert_id, gate = _route(logits, router_group_b, router_expert_b)
    plan = _moe_plan(expert_id)
    x_sorted = _dispatch_rows(hn, plan["row_src"], plan["n_rows"])
    y_sorted = _expert_ffn(x_sorted, plan, expert_w_gate, expert_w_up, expert_w_down)
    y = _combine(u, gate, y_sorted, plan["pos"])

    return (y[:n_p].reshape(batch, seq, d), y[n_p:].reshape(dec_batch, t_new, d),
            kv_prompt[0], kv_prompt[1], kv_prompt[2], shift_p, s_p.astype(state_wkv.dtype),
            kv_sample[0], kv_sample[1], kv_sample[2], shift_s, s_s.astype(state_wkv.dtype))
```
